```python
import math
import jax, jax.numpy as jnp
from jax import lax
import numpy as np

D_MODEL = 1024
BATCH = 2
SEQ = 8192
DEPTH = 2

D_FF = ((8 * D_MODEL // 3 + 127) // 128) * 128
D_CONV = D_MODEL
CONV_WIDTH = 31
CONV_GROUPS = 8
D_RNN = ((4 * D_MODEL // 3 + 127) // 128) * 128
RNN_BLOCKS = 16
RNN_BLOCK = D_RNN // RNN_BLOCKS
RNN_CONV_WIDTH = 4
RG_LRU_C = 8.0
LN_EPS = 1e-5
D_IN = 2 * D_CONV + 2 * D_RNN + 2 * D_MODEL
SPLITS = [D_CONV, 2 * D_CONV, 2 * D_CONV + D_RNN, 2 * D_CONV + 2 * D_RNN, 2 * D_CONV + 2 * D_RNN + D_MODEL]
DEEPNORM_ALPHA = (2 * DEPTH) ** 0.25
DEEPNORM_BETA = (8 * DEPTH) ** -0.25

kernel_name = "hybrid_conformer_conv_rglru_deepnorm"


def layer_norm(x, g, b):
    xf = x.astype(jnp.float32)
    mu = jnp.mean(xf, axis=-1, keepdims=True)
    xc = xf - mu
    var = jnp.mean(xc * xc, axis=-1, keepdims=True)
    y = xc * lax.rsqrt(var + LN_EPS) * g.astype(jnp.float32) + b.astype(jnp.float32)
    return y.astype(x.dtype)


def group_norm(x, g, b, groups):
    B, S, C = x.shape
    xf = x.astype(jnp.float32).reshape(B, S, groups, C // groups)
    mu = jnp.mean(xf, axis=-1, keepdims=True)
    xc = xf - mu
    var = jnp.mean(xc * xc, axis=-1, keepdims=True)
    y = (xc * lax.rsqrt(var + LN_EPS)).reshape(B, S, C)
    y = y * g.astype(jnp.float32) + b.astype(jnp.float32)
    return y.astype(x.dtype)


def swiglu_ffn(x, w_gu, w_down):
    gate, up = jnp.split(x @ w_gu, 2, axis=-1)
    return (jax.nn.silu(gate) * up) @ w_down


def causal_depthwise_conv(x, w, b):
    K, C = w.shape
    y = lax.conv_general_dilated(
        x, w[:, None, :].astype(x.dtype),
        window_strides=(1,), padding=[(K - 1, 0)],
        dimension_numbers=("NWC", "WIO", "NWC"),
        feature_group_count=C)
    return y + b


def block_diag_linear(x, w, b):
    B, S, _ = x.shape
    xh = x.reshape(B, S, RNN_BLOCKS, RNN_BLOCK)
    y = jnp.einsum("bshi,hij->bshj", xh, w)
    return y.reshape(B, S, D_RNN) + b


def rg_lru(x, w_a, b_a, w_x, b_x, lam):
    r = jax.nn.sigmoid(block_diag_linear(x, w_a, b_a)).astype(jnp.float32)
    i = jax.nn.sigmoid(block_diag_linear(x, w_x, b_x)).astype(jnp.float32)
    log_a = -RG_LRU_C * r * jax.nn.softplus(-lam.astype(jnp.float32))
    a = jnp.exp(log_a)
    mult = jnp.sqrt(-jnp.expm1(2.0 * log_a))
    u = mult * (i * x.astype(jnp.float32))

    def combine(left, right):
        a1, b1 = left
        a2, b2 = right
        return a1 * a2, a2 * b1 + b2

    _, h = lax.associative_scan(combine, (a, u), axis=1)
    return h.astype(x.dtype)


def hybrid_mixer(x, w_in, b_in, dw_w, dw_b, gn_g, gn_b, conv_w_proj,
                 rc_w, rc_b, w_a, b_a, w_x, b_x, lam, rnn_w_proj, w_out):
    u = x @ w_in + b_in
    c_val, c_gate, r_x, r_gate, g_c, g_r = jnp.split(u, SPLITS, axis=-1)
    c = c_val * jax.nn.sigmoid(c_gate)
    c = causal_depthwise_conv(c, dw_w, dw_b)
    c = jax.nn.silu(group_norm(c, gn_g, gn_b, CONV_GROUPS))
    y_conv = c @ conv_w_proj
    r = causal_depthwise_conv(r_x, rc_w, rc_b)
    h = rg_lru(r, w_a, b_a, w_x, b_x, lam)
    y_rnn = (h * jax.nn.gelu(r_gate)) @ rnn_w_proj
    m = jax.nn.sigmoid(g_c) * y_conv + jax.nn.sigmoid(g_r) * y_rnn
    return m @ w_out


def setup_inputs(seed: int = 0) -> dict:
    key = jax.random.key(seed)
    ks = jax.random.split(key, 32)
    L = DEPTH
    f32 = jnp.float32

    def nrm(k, shape, scale):
        return jax.random.normal(k, shape, f32) * scale

    def gain(k, shape):
        return 1.0 + nrm(k, shape, 0.02)

    u = jax.random.uniform(ks[18], (L, D_RNN), f32, minval=0.9, maxval=0.999)
    log_a = jnp.log(u) / RG_LRU_C
    lam = log_a - jnp.log(-jnp.expm1(log_a))

    return {
        "x": nrm(ks[0], (BATCH, SEQ, D_MODEL), 1.0),
        "ffn1_w_gu": nrm(ks[1], (L, D_MODEL, 2 * D_FF), D_MODEL ** -0.5),
        "ffn1_w_down": nrm(ks[2], (L, D_FF, D_MODEL), DEEPNORM_BETA * D_FF ** -0.5),
        "ln1_g": gain(ks[3], (L, D_MODEL)),
        "ln1_b": nrm(ks[4], (L, D_MODEL), 0.02),
        "mix_w_in": nrm(ks[5], (L, D_MODEL, D_IN), D_MODEL ** -0.5),
        "mix_b_in": nrm(ks[6], (L, D_IN), 0.02),
        "conv_dw_w": nrm(ks[7], (L, CONV_WIDTH, D_CONV), CONV_WIDTH ** -0.5),
        "conv_dw_b": nrm(ks[8], (L, D_CONV), 0.02),
        "conv_gn_g": gain(ks[9], (L, D_CONV)),
        "conv_gn_b": nrm(ks[10], (L, D_CONV), 0.02),
        "conv_w_proj": nrm(ks[11], (L, D_CONV, D_MODEL), DEEPNORM_BETA * D_CONV ** -0.5),
        "rnn_conv_w": nrm(ks[12], (L, RNN_CONV_WIDTH, D_RNN), RNN_CONV_WIDTH ** -0.5),
        "rnn_conv_b": nrm(ks[13], (L, D_RNN), 0.02),
        "rnn_w_a": nrm(ks[14], (L, RNN_BLOCKS, RNN_BLOCK, RNN_BLOCK), RNN_BLOCK ** -0.5),
        "rnn_b_a": nrm(ks[15], (L, D_RNN), 0.02),
        "rnn_w_x": nrm(ks[16], (L, RNN_BLOCKS, RNN_BLOCK, RNN_BLOCK), RNN_BLOCK ** -0.5),
        "rnn_b_x": nrm(ks[17], (L, D_RNN), 0.02),
        "rnn_lambda": lam,
        "rnn_w_proj": nrm(ks[19], (L, D_RNN, D_MODEL), DEEPNORM_BETA * D_RNN ** -0.5),
        "mix_w_out": nrm(ks[20], (L, D_MODEL, D_MODEL), DEEPNORM_BETA * D_MODEL ** -0.5),
        "ln2_g": gain(ks[21], (L, D_MODEL)),
        "ln2_b": nrm(ks[22], (L, D_MODEL), 0.02),
        "ffn2_w_gu": nrm(ks[23], (L, D_MODEL, 2 * D_FF), D_MODEL ** -0.5),
        "ffn2_w_down": nrm(ks[24], (L, D_FF, D_MODEL), DEEPNORM_BETA * D_FF ** -0.5),
        "ln3_g": gain(ks[25], (L, D_MODEL)),
        "ln3_b": nrm(ks[26], (L, D_MODEL), 0.02),
    }


def reference(x, ffn1_w_gu, ffn1_w_down, ln1_g, ln1_b, mix_w_in, mix_b_in,
              conv_dw_w, conv_dw_b, conv_gn_g, conv_gn_b, conv_w_proj,
              rnn_conv_w, rnn_conv_b, rnn_w_a, rnn_b_a, rnn_w_x, rnn_b_x,
              rnn_lambda, rnn_w_proj, mix_w_out, ln2_g, ln2_b,
              ffn2_w_gu, ffn2_w_down, ln3_g, ln3_b):
    for l in range(DEPTH):
        x = layer_norm(DEEPNORM_ALPHA * x + 0.5 * swiglu_ffn(x, ffn1_w_gu[l], ffn1_w_down[l]),
                       ln1_g[l], ln1_b[l])
        mix = hybrid_mixer(x, mix_w_in[l], mix_b_in[l], conv_dw_w[l], conv_dw_b[l],
                           conv_gn_g[l], conv_gn_b[l], conv_w_proj[l],
                           rnn_conv_w[l], rnn_conv_b[l], rnn_w_a[l], rnn_b_a[l],
                           rnn_w_x[l], rnn_b_x[l], rnn_lambda[l], rnn_w_proj[l], mix_w_out[l])
        x = layer_norm(DEEPNORM_ALPHA * x + mix, ln2_g[l], ln2_b[l])
        x = layer_norm(DEEPNORM_ALPHA * x + 0.5 * swiglu_ffn(x, ffn2_w_gu[l], ffn2_w_down[l]),
                       ln3_g[l], ln3_b[l])
    return x
```

```python
import functools

import jax
import jax.numpy as jnp
from jax import lax
from jax.experimental import pallas as pl
from jax.experimental.pallas import tpu as pltpu

F32 = jnp.float32
BF16 = jnp.bfloat16

LN_EPS = 1e-5
RG_LRU_C = 8.0
CONV_GROUPS = 8
SUBLANES = 8
LANES = 128
VMEM_LIMIT_BYTES = 56 * 1024 * 1024

FFN_TM = 512
MIX_TM = 256
CONV_ROWS = 64


def _layer_norm(z, g, b):
    mu = jnp.mean(z, axis=-1, keepdims=True)
    zc = z - mu
    var = jnp.mean(zc * zc, axis=-1, keepdims=True)
    return zc * lax.rsqrt(var + LN_EPS) * g + b


def _silu(v):
    return v * jax.nn.sigmoid(v)


def _dot(a, b):
    return jnp.dot(a, b, preferred_element_type=F32)


def _ffn_ln_kernel(x_ref, wgu_ref, wd_ref, g_ref, b_ref, o_ref, *, alpha, d_ff):
    x = x_ref[...]
    xb = x.astype(BF16)
    gate = _dot(xb, wgu_ref[:, :d_ff])
    up = _dot(xb, wgu_ref[:, d_ff:])
    act = (_silu(gate) * up).astype(BF16)
    y = _dot(act, wd_ref[...])
    z = alpha * x + 0.5 * y
    o_ref[...] = _layer_norm(z, g_ref[...], b_ref[...])


def _resident(shape, layer):
    nd = len(shape)
    return pl.BlockSpec((None,) + tuple(shape), lambda *_: (layer,) + (0,) * nd,
                        pipeline_mode=pl.Buffered(1))


def _ffn_ln(x2d, wgu, wd, g, b, layer, alpha):
    m, d = x2d.shape
    d_ff = wd.shape[1]
    tm = FFN_TM
    return pl.pallas_call(
        functools.partial(_ffn_ln_kernel, alpha=alpha, d_ff=d_ff),
        grid=(m // tm,),
        in_specs=[
            pl.BlockSpec((tm, d), lambda i: (i, 0)),
            _resident((d, 2 * d_ff), layer),
            _resident((d_ff, d), layer),
            _resident((1, d), layer),
            _resident((1, d), layer),
        ],
        out_specs=pl.BlockSpec((tm, d), lambda i: (i, 0)),
        out_shape=jax.ShapeDtypeStruct((m, d), F32),
        compiler_params=pltpu.CompilerParams(
            dimension_semantics=("arbitrary",),
            vmem_limit_bytes=VMEM_LIMIT_BYTES),
        name="ffn_ln",
    )(x2d, wgu, wd, g, b)


CONV_HALO = 32
RNN_HALO = 8


def _mixer_kernel(x_ref, w_in_ref, b_in_ref, dw_w_ref, dw_b_ref, gn_g_ref, gn_b_ref,
                  cproj_ref, rc_w_ref, rc_b_ref, wa_ref, ba_ref, wx_ref, bx_ref,
                  lam_ref, rproj_ref, wout_ref, ln_g_ref, ln_b_ref, o_ref,
                  cbuf, ybuf, rbuf, rcv, gabuf, gxbuf, rgbuf, hgbuf, hcar,
                  *, alpha, tm, d_conv, d_rnn, d_model, conv_w, rnn_conv_w):
    s = pl.program_id(1)

    @pl.when(s == 0)
    def _():
        cbuf[0:CONV_HALO, :] = jnp.zeros((CONV_HALO, d_conv), F32)
        rbuf[0:RNN_HALO, :] = jnp.zeros((RNN_HALO, d_rnn), F32)
        hcar[...] = jnp.zeros((SUBLANES, d_rnn), F32)

    x = x_ref[...]
    xb = x.astype(BF16)

    def proj(lo, hi):
        return _dot(xb, w_in_ref[:, lo:hi]) + b_in_ref[:, lo:hi]

    o_cg = d_conv
    o_rx = 2 * d_conv
    o_rg = o_rx + d_rnn
    o_gc = o_rg + d_rnn
    o_gr = o_gc + d_model

    cbuf[CONV_HALO:CONV_HALO + tm, :] = proj(0, o_cg) * jax.nn.sigmoid(proj(o_cg, o_rx))

    lead = CONV_HALO - (conv_w - 1)
    R = CONV_ROWS
    gsz = d_conv // CONV_GROUPS

    def conv_chunk(i, carry):
        r0 = pl.multiple_of(i * R, R)
        for j in range(d_conv // gsz):
            ln = slice(j * gsz, (j + 1) * gsz)
            win = cbuf[pl.ds(r0, R + CONV_HALO), ln]
            acc = jnp.broadcast_to(dw_b_ref[:, ln], (R, gsz))
            for sub in range(SUBLANES):
                shifted = win[sub:sub + R + CONV_HALO - SUBLANES]
                for blk in range(CONV_HALO // SUBLANES):
                    k = blk * SUBLANES + sub - lead
                    if 0 <= k < conv_w:
                        acc = acc + dw_w_ref[k:k + 1, ln] * shifted[blk * SUBLANES:blk * SUBLANES + R]
            k = CONV_HALO - lead
            if k < conv_w:
                acc = acc + dw_w_ref[k:k + 1, ln] * win[CONV_HALO:CONV_HALO + R]
            mu = jnp.mean(acc, axis=-1, keepdims=True)
            xc = acc - mu
            var = jnp.mean(xc * xc, axis=-1, keepdims=True)
            yn = xc * lax.rsqrt(var + LN_EPS) * gn_g_ref[:, ln] + gn_b_ref[:, ln]
            ybuf[pl.ds(r0, R), ln] = _silu(yn)
        return carry

    lax.fori_loop(0, tm // R, conv_chunk, 0)
    cbuf[0:CONV_HALO, :] = cbuf[tm:tm + CONV_HALO, :]
    y_conv = _dot(ybuf[...].astype(BF16), cproj_ref[...])

    rbuf[RNN_HALO:RNN_HALO + tm, :] = proj(o_rx, o_rg)
    rlead = RNN_HALO - (rnn_conv_w - 1)

    def rconv_chunk(i, carry):
        r0 = pl.multiple_of(i * R, R)
        for j in range(d_rnn // LANES):
            ln = slice(j * LANES, (j + 1) * LANES)
            win = rbuf[pl.ds(r0, R + RNN_HALO), ln]
            acc = jnp.broadcast_to(rc_b_ref[:, ln], (R, LANES))
            for k in range(rnn_conv_w):
                acc = acc + rc_w_ref[k:k + 1, ln] * win[rlead + k:rlead + k + R]
            rcv[pl.ds(r0, R), ln] = acc
        return carry

    lax.fori_loop(0, tm // R, rconv_chunk, 0)
    rbuf[0:RNN_HALO, :] = rbuf[tm:tm + RNN_HALO, :]

    rb = rcv[...].astype(BF16)
    gabuf[...] = _dot(rb, wa_ref[...]) + ba_ref[...]
    gxbuf[...] = _dot(rb, wx_ref[...]) + bx_ref[...]
    rgbuf[...] = proj(o_rg, o_gc)

    neg_c_sp = -RG_LRU_C * jax.nn.softplus(-lam_ref[...])
    row = lax.broadcasted_iota(jnp.int32, (SUBLANES, d_rnn), 0)

    def scan_chunk(i, h_prev):
        r0 = pl.multiple_of(i * SUBLANES, SUBLANES)
        rows = pl.ds(r0, SUBLANES)
        rgate = jax.nn.sigmoid(gabuf[rows, :])
        igate = jax.nn.sigmoid(gxbuf[rows, :])
        log_a = rgate * neg_c_sp
        a = jnp.exp(log_a)
        mult = jnp.sqrt(-jnp.tanh(log_a) * (a * a + 1.0))
        u = mult * (igate * rcv[rows, :])
        d = 1
        while d < SUBLANES:
            keep = row >= d
            a_s = jnp.where(keep, pltpu.roll(a, d, 0), 1.0)
            u_s = jnp.where(keep, pltpu.roll(u, d, 0), 0.0)
            u = a * u_s + u
            a = a * a_s
            d *= 2
        h = u + a * h_prev
        hgbuf[rows, :] = h * jax.nn.gelu(rgbuf[rows, :])
        return jnp.broadcast_to(h[SUBLANES - 1:SUBLANES, :], (SUBLANES, d_rnn))

    hcar[...] = lax.fori_loop(0, tm // SUBLANES, scan_chunk, hcar[...])
    y_rnn = _dot(hgbuf[...].astype(BF16), rproj_ref[...])

    m = jax.nn.sigmoid(proj(o_gc, o_gr)) * y_conv + jax.nn.sigmoid(proj(o_gr, o_gr + d_model)) * y_rnn
    mix = _dot(m.astype(BF16), wout_ref[...])
    z = alpha * x + mix
    o_ref[...] = _layer_norm(z, ln_g_ref[...], ln_b_ref[...])


def _mixer(x3d, p, layer, alpha):
    bsz, seq, d = x3d.shape
    tm = MIX_TM
    d_conv = p["dw_w"].shape[-1]
    d_rnn = p["rc_w"].shape[-1]
    conv_w = p["dw_w"].shape[1]
    rnn_conv_w = p["rc_w"].shape[1]
    d_in = p["w_in"].shape[-1]
    names = ["w_in", "b_in", "dw_w", "dw_b", "gn_g", "gn_b", "cproj", "rc_w", "rc_b",
             "wa", "ba", "wx", "bx", "lam", "rproj", "wout", "ln_g", "ln_b"]
    args = [p[n] for n in names]
    in_specs = [pl.BlockSpec((None, tm, d), lambda b, s: (b, s, 0))]
    in_specs += [_resident(a.shape[1:], layer) for a in args]
    kern = functools.partial(
        _mixer_kernel, alpha=alpha, tm=tm, d_conv=d_conv, d_rnn=d_rnn, d_model=d,
        conv_w=conv_w, rnn_conv_w=rnn_conv_w)
    del d_in
    return pl.pallas_call(
        kern,
        grid=(bsz, seq // tm),
        in_specs=in_specs,
        out_specs=pl.BlockSpec((None, tm, d), lambda b, s: (b, s, 0)),
        out_shape=jax.ShapeDtypeStruct((bsz, seq, d), F32),
        scratch_shapes=[
            pltpu.VMEM((tm + CONV_HALO, d_conv), F32),
            pltpu.VMEM((tm, d_conv), F32),
            pltpu.VMEM((tm + RNN_HALO, d_rnn), F32),
            pltpu.VMEM((tm, d_rnn), F32),
            pltpu.VMEM((tm, d_rnn), F32),
            pltpu.VMEM((tm, d_rnn), F32),
            pltpu.VMEM((tm, d_rnn), F32),
            pltpu.VMEM((tm, d_rnn), F32),
            pltpu.VMEM((SUBLANES, d_rnn), F32),
        ],
        compiler_params=pltpu.CompilerParams(
            dimension_semantics=("arbitrary", "arbitrary"),
            vmem_limit_bytes=VMEM_LIMIT_BYTES),
        name="mixer",
    )(x3d, *args)


def _block_diag_dense(w):
    h, bk, _ = w.shape
    eye = jnp.eye(h, dtype=w.dtype)
    return jnp.einsum("hij,hg->higj", w, eye).reshape(h * bk, h * bk)


def kernel(x, ffn1_w_gu, ffn1_w_down, ln1_g, ln1_b, mix_w_in, mix_b_in, conv_dw_w, conv_dw_b, conv_gn_g, conv_gn_b, conv_w_proj, rnn_conv_w, rnn_conv_b, rnn_w_a, rnn_b_a, rnn_w_x, rnn_b_x, rnn_lambda, rnn_w_proj, mix_w_out, ln2_g, ln2_b, ffn2_w_gu, ffn2_w_down, ln3_g, ln3_b):
    bsz, seq, d = x.shape
    depth = ffn1_w_gu.shape[0]
    alpha = float((2 * depth) ** 0.25)

    def row(v):
        return v[:, None, :]

    ffn1_wgu = ffn1_w_gu.astype(BF16)
    ffn1_wd = ffn1_w_down.astype(BF16)
    ffn2_wgu = ffn2_w_gu.astype(BF16)
    ffn2_wd = ffn2_w_down.astype(BF16)
    mp = {
        "w_in": mix_w_in.astype(BF16), "b_in": row(mix_b_in),
        "dw_w": conv_dw_w, "dw_b": row(conv_dw_b),
        "gn_g": row(conv_gn_g), "gn_b": row(conv_gn_b),
        "cproj": conv_w_proj.astype(BF16),
        "rc_w": rnn_conv_w, "rc_b": row(rnn_conv_b),
        "wa": jax.vmap(_block_diag_dense)(rnn_w_a).astype(BF16), "ba": row(rnn_b_a),
        "wx": jax.vmap(_block_diag_dense)(rnn_w_x).astype(BF16), "bx": row(rnn_b_x),
        "lam": row(rnn_lambda),
        "rproj": rnn_w_proj.astype(BF16), "wout": mix_w_out.astype(BF16),
        "ln_g": row(ln2_g), "ln_b": row(ln2_b),
    }
    ln1g, ln1b, ln3g, ln3b = row(ln1_g), row(ln1_b), row(ln3_g), row(ln3_b)

    for l in range(depth):
        h = _ffn_ln(x.reshape(bsz * seq, d), ffn1_wgu, ffn1_wd, ln1g, ln1b, l, alpha)
        h = _mixer(h.reshape(bsz, seq, d), mp, l, alpha)
        h = _ffn_ln(h.reshape(bsz * seq, d), ffn2_wgu, ffn2_wd, ln3g, ln3b, l, alpha)
        x = h.reshape(bsz, seq, d)
    return x
```

```python
import functools

import jax
import jax.numpy as jnp
from jax import lax
from jax.experimental import pallas as pl
from jax.experimental.pallas import tpu as pltpu

F32 = jnp.float32
BF16 = jnp.bfloat16

LN_EPS = 1e-5
RG_LRU_C = 8.0
CONV_GROUPS = 8
SUBLANES = 8
LANES = 128
MXU_COLS = 256
VMEM_LIMIT_BYTES = 56 * 1024 * 1024

FFN_TM = 512
MIX_TM = 256
CONV_ROWS = 64


def _layer_norm(z, g, b):
    mu = jnp.mean(z, axis=-1, keepdims=True)
    zc = z - mu
    var = jnp.mean(zc * zc, axis=-1, keepdims=True)
    return zc * lax.rsqrt(var + LN_EPS) * g + b


def _silu(v):
    return v * jax.nn.sigmoid(v)


def _dot(a, b):
    return jnp.dot(a, b, preferred_element_type=F32)


def _ffn_ln_kernel(x_ref, wgu_ref, wd_ref, g_ref, b_ref, o_ref, *, alpha, d_ff):
    x = x_ref[...]
    xb = x.astype(BF16)
    gate = _dot(xb, wgu_ref[:, :d_ff])
    up = _dot(xb, wgu_ref[:, d_ff:])
    act = (_silu(gate) * up).astype(BF16)
    y = _dot(act, wd_ref[...])
    z = alpha * x + 0.5 * y
    o_ref[...] = _layer_norm(z, g_ref[...], b_ref[...])


def _resident(shape, layer):
    nd = len(shape)
    return pl.BlockSpec((None,) + tuple(shape), lambda *_: (layer,) + (0,) * nd,
                        pipeline_mode=pl.Buffered(1))


def _ffn_ln(x2d, wgu, wd, g, b, layer, alpha):
    m, d = x2d.shape
    d_ff = wd.shape[1]
    tm = FFN_TM
    return pl.pallas_call(
        functools.partial(_ffn_ln_kernel, alpha=alpha, d_ff=d_ff),
        grid=(m // tm,),
        in_specs=[
            pl.BlockSpec((tm, d), lambda i: (i, 0)),
            _resident((d, 2 * d_ff), layer),
            _resident((d_ff, d), layer),
            _resident((1, d), layer),
            _resident((1, d), layer),
        ],
        out_specs=pl.BlockSpec((tm, d), lambda i: (i, 0)),
        out_shape=jax.ShapeDtypeStruct((m, d), F32),
        compiler_params=pltpu.CompilerParams(
            dimension_semantics=("arbitrary",),
            vmem_limit_bytes=VMEM_LIMIT_BYTES),
        name="ffn_ln",
    )(x2d, wgu, wd, g, b)


CONV_HALO = 32
RNN_HALO = 8


def _gate_windows(n_blocks, block):
    d = n_blocks * block
    assert d % LANES == 0
    spans = []
    for j in range(d // LANES):
        b0 = (j * LANES) // block
        b1 = (j * LANES + LANES - 1) // block
        spans.append(((b0 * block) // LANES * LANES, (b1 + 1) * block))
    k_win = min(max(-(-(e - s) // LANES) * LANES for s, e in spans), d)
    return tuple(min(s, d - k_win) for s, _ in spans), k_win


def _mixer_kernel(x_ref, w_in_ref, b_in_ref, dw_w_ref, dw_b_ref, gn_g_ref, gn_b_ref,
                  cproj_ref, rc_w_ref, rc_b_ref, wg_ref, ba_ref, bx_ref,
                  lam_ref, rproj_ref, wout_ref, ln_g_ref, ln_b_ref, o_ref,
                  cph, ybuf, rbuf, rcv, gabuf, gxbuf, rgbuf, hgbuf, gcbuf, grbuf, hcar,
                  *, alpha, tm, d_conv, d_rnn, d_model, conv_w, rnn_conv_w, gate_starts):
    s = pl.program_id(1)

    @pl.when(s == 0)
    def _():
        cph[...] = jnp.zeros(cph.shape, F32)
        rbuf[0:RNN_HALO, :] = jnp.zeros((RNN_HALO, d_rnn), F32)
        hcar[...] = jnp.zeros((SUBLANES, d_rnn), F32)

    x = x_ref[...]
    xb = x.astype(BF16)

    def proj(lo, hi):
        return _dot(xb, w_in_ref[:, lo:hi]) + b_in_ref[:, lo:hi]

    o_cg = d_conv
    o_rx = 2 * d_conv
    o_rg = o_rx + d_rnn
    o_gc = o_rg + d_rnn
    o_gr = o_gc + d_model

    c = proj(0, o_cg) * jax.nn.sigmoid(proj(o_cg, o_rx))
    for p in range(SUBLANES):
        cph[p, CONV_HALO - p:CONV_HALO - p + tm, :] = c

    def emit_rx(lo, hi):
        rbuf[RNN_HALO:RNN_HALO + tm, lo:hi] = proj(o_rx + lo, o_rx + hi)

    def emit_rg(lo, hi):
        rgbuf[:, lo:hi] = proj(o_rg + lo, o_rg + hi)

    def emit_gc(lo, hi):
        gcbuf[:, lo:hi] = proj(o_gc + lo, o_gc + hi)

    def emit_gr(lo, hi):
        grbuf[:, lo:hi] = proj(o_gr + lo, o_gr + hi)

    mxu_units = []
    for emit, width in ((emit_rx, d_rnn), (emit_rg, d_rnn), (emit_gc, d_model), (emit_gr, d_model)):
        for lo in range(0, width, MXU_COLS):
            mxu_units.append(functools.partial(emit, lo, min(lo + MXU_COLS, width)))

    lead = CONV_HALO - (conv_w - 1)
    R = CONV_ROWS
    gsz = d_conv // CONV_GROUPS

    def conv_unit(r0, j):
        ln = slice(j * gsz, (j + 1) * gsz)
        acc = jnp.broadcast_to(dw_b_ref[:, ln], (R, gsz))
        for k in range(conv_w):
            off = lead + k
            p, a = off % SUBLANES, off // SUBLANES * SUBLANES
            acc = acc + dw_w_ref[k:k + 1, ln] * cph[p, r0 + a:r0 + a + R, ln]
        mu = jnp.mean(acc, axis=-1, keepdims=True)
        xc = acc - mu
        var = jnp.mean(xc * xc, axis=-1, keepdims=True)
        yn = xc * lax.rsqrt(var + LN_EPS) * gn_g_ref[:, ln] + gn_b_ref[:, ln]
        ybuf[r0:r0 + R, ln] = _silu(yn).astype(BF16)

    conv_units = [(r0, j) for r0 in range(0, tm, R) for j in range(CONV_GROUPS)]
    n_c, n_m = len(conv_units), len(mxu_units)
    done_m = 0
    for i, (r0, j) in enumerate(conv_units):
        conv_unit(r0, j)
        while done_m * n_c < (i + 1) * n_m:
            mxu_units[done_m]()
            done_m += 1

    for p in range(SUBLANES):
        cph[p, 0:CONV_HALO, :] = cph[p, tm:tm + CONV_HALO, :]
    y_conv = _dot(ybuf[...], cproj_ref[...])

    rlead = RNN_HALO - (rnn_conv_w - 1)

    def rconv_chunk(i, carry):
        r0 = pl.multiple_of(i * R, R)
        for j in range(d_rnn // LANES):
            ln = slice(j * LANES, (j + 1) * LANES)
            win = rbuf[pl.ds(r0, R + RNN_HALO), ln]
            acc = jnp.broadcast_to(rc_b_ref[:, ln], (R, LANES))
            for k in range(rnn_conv_w):
                acc = acc + rc_w_ref[k:k + 1, ln] * win[rlead + k:rlead + k + R]
            rcv[pl.ds(r0, R), ln] = acc
        return carry

    lax.fori_loop(0, tm // R, rconv_chunk, 0)
    rbuf[0:RNN_HALO, :] = rbuf[tm:tm + RNN_HALO, :]

    rb = rcv[...].astype(BF16)
    k_win = wg_ref.shape[1]
    for j, ks in enumerate(gate_starts):
        ln = slice(j * LANES, (j + 1) * LANES)
        g2 = _dot(rb[:, ks:ks + k_win], wg_ref[j])
        gabuf[:, ln] = g2[:, :LANES] + ba_ref[:, ln]
        gxbuf[:, ln] = g2[:, LANES:] + bx_ref[:, ln]

    neg_c_sp = -RG_LRU_C * jax.nn.softplus(-lam_ref[...])
    row = lax.broadcasted_iota(jnp.int32, (SUBLANES, d_rnn), 0)

    def scan_chunk(i, h_prev):
        r0 = pl.multiple_of(i * SUBLANES, SUBLANES)
        rows = pl.ds(r0, SUBLANES)
        rgate = jax.nn.sigmoid(gabuf[rows, :])
        igate = jax.nn.sigmoid(gxbuf[rows, :])
        log_a = rgate * neg_c_sp
        a = jnp.exp(log_a)
        mult = jnp.sqrt(-jnp.tanh(log_a) * (a * a + 1.0))
        u = mult * (igate * rcv[rows, :])
        d = 1
        while d < SUBLANES:
            keep = row >= d
            a_s = jnp.where(keep, pltpu.roll(a, d, 0), 1.0)
            u_s = jnp.where(keep, pltpu.roll(u, d, 0), 0.0)
            u = a * u_s + u
            a = a * a_s
            d *= 2
        h = u + a * h_prev
        hgbuf[rows, :] = h * jax.nn.gelu(rgbuf[rows, :])
        return jnp.broadcast_to(h[SUBLANES - 1:SUBLANES, :], (SUBLANES, d_rnn))

    hcar[...] = lax.fori_loop(0, tm // SUBLANES, scan_chunk, hcar[...])
    y_rnn = _dot(hgbuf[...].astype(BF16), rproj_ref[...])

    m = jax.nn.sigmoid(gcbuf[...]) * y_conv + jax.nn.sigmoid(grbuf[...]) * y_rnn
    mix = _dot(m.astype(BF16), wout_ref[...])
    z = alpha * x + mix
    o_ref[...] = _layer_norm(z, ln_g_ref[...], ln_b_ref[...])


def _mixer(x3d, p, layer, alpha, gate_starts):
    bsz, seq, d = x3d.shape
    tm = MIX_TM
    d_conv = p["dw_w"].shape[-1]
    d_rnn = p["rc_w"].shape[-1]
    conv_w = p["dw_w"].shape[1]
    rnn_conv_w = p["rc_w"].shape[1]
    assert conv_w - 1 <= CONV_HALO and rnn_conv_w - 1 <= RNN_HALO
    names = ["w_in", "b_in", "dw_w", "dw_b", "gn_g", "gn_b", "cproj", "rc_w", "rc_b",
             "wg", "ba", "bx", "lam", "rproj", "wout", "ln_g", "ln_b"]
    args = [p[n] for n in names]
    in_specs = [pl.BlockSpec((None, tm, d), lambda b, s: (b, s, 0))]
    in_specs += [_resident(a.shape[1:], layer) for a in args]
    kern = functools.partial(
        _mixer_kernel, alpha=alpha, tm=tm, d_conv=d_conv, d_rnn=d_rnn, d_model=d,
        conv_w=conv_w, rnn_conv_w=rnn_conv_w, gate_starts=gate_starts)
    return pl.pallas_call(
        kern,
        grid=(bsz, seq // tm),
        in_specs=in_specs,
        out_specs=pl.BlockSpec((None, tm, d), lambda b, s: (b, s, 0)),
        out_shape=jax.ShapeDtypeStruct((bsz, seq, d), F32),
        scratch_shapes=[
            pltpu.VMEM((SUBLANES, tm + CONV_HALO, d_conv), F32),
            pltpu.VMEM((tm, d_conv), BF16),
            pltpu.VMEM((tm + RNN_HALO, d_rnn), F32),
            pltpu.VMEM((tm, d_rnn), F32),
            pltpu.VMEM((tm, d_rnn), F32),
            pltpu.VMEM((tm, d_rnn), F32),
            pltpu.VMEM((tm, d_rnn), F32),
            pltpu.VMEM((tm, d_rnn), F32),
            pltpu.VMEM((tm, d), F32),
            pltpu.VMEM((tm, d), F32),
            pltpu.VMEM((SUBLANES, d_rnn), F32),
        ],
        compiler_params=pltpu.CompilerParams(
            dimension_semantics=("arbitrary", "arbitrary"),
            vmem_limit_bytes=VMEM_LIMIT_BYTES),
        name="mixer",
    )(x3d, *args)


def _block_diag_dense(w):
    h, bk, _ = w.shape
    eye = jnp.eye(h, dtype=w.dtype)
    return jnp.einsum("hij,hg->higj", w, eye).reshape(h * bk, h * bk)


def _gate_weights(w_a, w_x, gate_starts, k_win):
    wa = _block_diag_dense(w_a)
    wx = _block_diag_dense(w_x)
    chunks = []
    for j, ks in enumerate(gate_starts):
        cols = slice(j * LANES, (j + 1) * LANES)
        chunks.append(jnp.concatenate([wa[ks:ks + k_win, cols], wx[ks:ks + k_win, cols]], axis=1))
    return jnp.stack(chunks).astype(BF16)


def kernel(x, ffn1_w_gu, ffn1_w_down, ln1_g, ln1_b, mix_w_in, mix_b_in, conv_dw_w, conv_dw_b, conv_gn_g, conv_gn_b, conv_w_proj, rnn_conv_w, rnn_conv_b, rnn_w_a, rnn_b_a, rnn_w_x, rnn_b_x, rnn_lambda, rnn_w_proj, mix_w_out, ln2_g, ln2_b, ffn2_w_gu, ffn2_w_down, ln3_g, ln3_b):
    bsz, seq, d = x.shape
    depth = ffn1_w_gu.shape[0]
    alpha = float((2 * depth) ** 0.25)
    n_blocks, block = rnn_w_a.shape[1], rnn_w_a.shape[2]
    gate_starts, k_win = _gate_windows(n_blocks, block)

    def row(v):
        return v[:, None, :]

    ffn1_wgu = ffn1_w_gu.astype(BF16)
    ffn1_wd = ffn1_w_down.astype(BF16)
    ffn2_wgu = ffn2_w_gu.astype(BF16)
    ffn2_wd = ffn2_w_down.astype(BF16)
    mp = {
        "w_in": mix_w_in.astype(BF16), "b_in": row(mix_b_in),
        "dw_w": conv_dw_w, "dw_b": row(conv_dw_b),
        "gn_g": row(conv_gn_g), "gn_b": row(conv_gn_b),
        "cproj": conv_w_proj.astype(BF16),
        "rc_w": rnn_conv_w, "rc_b": row(rnn_conv_b),
        "wg": jax.vmap(lambda a, b: _gate_weights(a, b, gate_starts, k_win))(rnn_w_a, rnn_w_x),
        "ba": row(rnn_b_a), "bx": row(rnn_b_x),
        "lam": row(rnn_lambda),
        "rproj": rnn_w_proj.astype(BF16), "wout": mix_w_out.astype(BF16),
        "ln_g": row(ln2_g), "ln_b": row(ln2_b),
    }
    ln1g, ln1b, ln3g, ln3b = row(ln1_g), row(ln1_b), row(ln3_g), row(ln3_b)

    for l in range(depth):
        h = _ffn_ln(x.reshape(bsz * seq, d), ffn1_wgu, ffn1_wd, ln1g, ln1b, l, alpha)
        h = _mixer(h.reshape(bsz, seq, d), mp, l, alpha, gate_starts)
        h = _ffn_ln(h.reshape(bsz * seq, d), ffn2_wgu, ffn2_wd, ln3g, ln3b, l, alpha)
        x = h.reshape(bsz, seq, d)
    return x
```

```python
import functools

import jax
import jax.numpy as jnp
from jax import lax
from jax.experimental import pallas as pl
from jax.experimental.pallas import tpu as pltpu

F32 = jnp.float32
BF16 = jnp.bfloat16

LN_EPS = 1e-5
RG_LRU_C = 8.0
CONV_GROUPS = 8
SUBLANES = 8
LANES = 128
MXU_COLS = 256
VMEM_LIMIT_BYTES = 56 * 1024 * 1024

FFN_TM = 512
MIX_TM = 256
CONV_ROWS = 64


def _layer_norm(z, g, b):
    mu = jnp.mean(z, axis=-1, keepdims=True)
    zc = z - mu
    var = jnp.mean(zc * zc, axis=-1, keepdims=True)
    return zc * lax.rsqrt(var + LN_EPS) * g + b


def _silu(v):
    return v * jax.nn.sigmoid(v)


def _dot(a, b):
    return jnp.dot(a, b, preferred_element_type=F32)


def _ffn_ln_kernel(x_ref, wgu_ref, wd_ref, g_ref, b_ref, o_ref, *, alpha, d_ff):
    x = x_ref[...]
    xb = x.astype(BF16)
    gate = _dot(xb, wgu_ref[:, :d_ff])
    up = _dot(xb, wgu_ref[:, d_ff:])
    act = (_silu(gate) * up).astype(BF16)
    y = _dot(act, wd_ref[...])
    z = alpha * x + 0.5 * y
    o_ref[...] = _layer_norm(z, g_ref[...], b_ref[...])


def _resident(shape, layer):
    nd = len(shape)
    return pl.BlockSpec((None,) + tuple(shape), lambda *_: (layer,) + (0,) * nd,
                        pipeline_mode=pl.Buffered(1))


def _ffn_ln(x2d, wgu, wd, g, b, layer, alpha):
    m, d = x2d.shape
    d_ff = wd.shape[1]
    tm = FFN_TM
    return pl.pallas_call(
        functools.partial(_ffn_ln_kernel, alpha=alpha, d_ff=d_ff),
        grid=(m // tm,),
        in_specs=[
            pl.BlockSpec((tm, d), lambda i: (i, 0)),
            _resident((d, 2 * d_ff), layer),
            _resident((d_ff, d), layer),
            _resident((1, d), layer),
            _resident((1, d), layer),
        ],
        out_specs=pl.BlockSpec((tm, d), lambda i: (i, 0)),
        out_shape=jax.ShapeDtypeStruct((m, d), F32),
        compiler_params=pltpu.CompilerParams(
            dimension_semantics=("arbitrary",),
            vmem_limit_bytes=VMEM_LIMIT_BYTES),
        name="ffn_ln",
    )(x2d, wgu, wd, g, b)


CONV_HALO = 32
RNN_HALO = 8
GROUPS_PER_ITER = MXU_COLS // LANES


def _gate_windows(n_blocks, block):
    d = n_blocks * block
    assert d % LANES == 0
    spans = []
    for j in range(d // LANES):
        b0 = (j * LANES) // block
        b1 = (j * LANES + LANES - 1) // block
        spans.append(((b0 * block) // LANES * LANES, (b1 + 1) * block))
    k_win = min(max(-(-(e - s) // LANES) * LANES for s, e in spans), d)
    return tuple(min(s, d - k_win) for s, _ in spans), k_win


def _mixer_kernel(x_ref, wc_ref, bc_ref, wf_ref, bf_ref, dw_w_ref, dw_b_ref, gn_g_ref, gn_b_ref,
                  cproj_ref, rc_w_ref, rc_b_ref, wg_ref, ba_ref, bx_ref,
                  lam_ref, rproj_ref, wout_ref, ln_g_ref, ln_b_ref, o_ref,
                  xbuf, cph, ybuf, ubuf, rbuf, rcv, gabuf, gxbuf, hgbuf, hcar,
                  *, alpha, tm, d_conv, d_rnn, d_model, conv_w, rnn_conv_w, gate_starts):
    s = pl.program_id(1)
    n_groups = d_conv // LANES
    n_rt = d_rnn // LANES
    n_mt = d_model // LANES

    @pl.when(s == 0)
    def _():
        cph[...] = jnp.zeros(cph.shape, F32)
        rbuf[0:RNN_HALO, :] = jnp.zeros((RNN_HALO, d_rnn), F32)
        hcar[...] = jnp.zeros((SUBLANES, d_rnn), F32)

    x = x_ref[...]
    xb = x.astype(BF16)
    xbuf[...] = xb

    def proj(lo, hi):
        return _dot(xb, wc_ref[:, lo:hi]) + bc_ref[:, lo:hi]

    o_cg = d_conv
    o_rx = 2 * d_conv

    c = proj(0, o_cg) * jax.nn.sigmoid(proj(o_cg, o_rx))
    for p in range(SUBLANES):
        for g in range(n_groups):
            cph[p * n_groups + g, CONV_HALO - p:CONV_HALO - p + tm, :] = c[:, g * LANES:(g + 1) * LANES]

    rbuf[RNN_HALO:RNN_HALO + tm, :] = proj(o_rx, o_rx + d_rnn)
    rlead = RNN_HALO - (rnn_conv_w - 1)
    R = CONV_ROWS
    for t in range(n_rt):
        ln = slice(t * LANES, (t + 1) * LANES)
        for r0 in range(0, tm, R):
            win = rbuf[r0:r0 + R + RNN_HALO, ln]
            acc = jnp.broadcast_to(rc_b_ref[:, ln], (R, LANES))
            for k in range(rnn_conv_w):
                acc = acc + rc_w_ref[k:k + 1, ln] * win[rlead + k:rlead + k + R]
            rcv[r0:r0 + R, ln] = acc
    rbuf[0:RNN_HALO, :] = rbuf[tm:tm + RNN_HALO, :]

    lead = CONV_HALO - (conv_w - 1)
    fill_tiles = wf_ref.shape[-1] // LANES

    def conv_iter(i, carry):
        for g in range(GROUPS_PER_ITER):
            grp = i * GROUPS_PER_ITER + g
            for r0 in range(0, tm, R):
                acc = jnp.broadcast_to(dw_b_ref[grp], (R, LANES))
                for k in range(conv_w):
                    off = lead + k
                    p, a = off % SUBLANES, off // SUBLANES * SUBLANES
                    acc = acc + dw_w_ref[grp, k:k + 1, :] * cph[p * n_groups + grp, r0 + a:r0 + a + R, :]
                mu = jnp.mean(acc, axis=-1, keepdims=True)
                xc = acc - mu
                var = jnp.mean(xc * xc, axis=-1, keepdims=True)
                yn = xc * lax.rsqrt(var + LN_EPS) * gn_g_ref[grp] + gn_b_ref[grp]
                ybuf[i, r0:r0 + R, g * LANES:(g + 1) * LANES] = _silu(yn).astype(BF16)
        u = _dot(xbuf[...], wf_ref[i]) + bf_ref[i]
        for t in range(fill_tiles):
            ubuf[i * fill_tiles + t] = u[:, t * LANES:(t + 1) * LANES]
        return carry

    lax.fori_loop(0, n_groups // GROUPS_PER_ITER, conv_iter, 0)
    for q in range(SUBLANES * n_groups):
        cph[q, 0:CONV_HALO, :] = cph[q, tm:tm + CONV_HALO, :]

    y_conv = _dot(ybuf[0], cproj_ref[0])
    for q in range(1, n_groups // GROUPS_PER_ITER):
        y_conv = y_conv + _dot(ybuf[q], cproj_ref[q])

    rb = rcv[...].astype(BF16)
    k_win = wg_ref.shape[1]
    for j, ks in enumerate(gate_starts):
        ln = slice(j * LANES, (j + 1) * LANES)
        g2 = _dot(rb[:, ks:ks + k_win], wg_ref[j])
        gabuf[:, ln] = g2[:, :LANES] + ba_ref[:, ln]
        gxbuf[:, ln] = g2[:, LANES:] + bx_ref[:, ln]

    neg_c_sp = -RG_LRU_C * jax.nn.softplus(-lam_ref[...])
    row = lax.broadcasted_iota(jnp.int32, (SUBLANES, LANES), 0)
    for t in range(n_rt):
        ln = slice(t * LANES, (t + 1) * LANES)
        ncs = neg_c_sp[:, ln]
        h_prev = hcar[:, ln]
        for r0 in range(0, tm, SUBLANES):
            rows = slice(r0, r0 + SUBLANES)
            rgate = jax.nn.sigmoid(gabuf[rows, ln])
            igate = jax.nn.sigmoid(gxbuf[rows, ln])
            log_a = rgate * ncs
            a = jnp.exp(log_a)
            mult = jnp.sqrt(-jnp.tanh(log_a) * (a * a + 1.0))
            u = mult * (igate * rcv[rows, ln])
            d = 1
            while d < SUBLANES:
                keep = row >= d
                a_s = jnp.where(keep, pltpu.roll(a, d, 0), 1.0)
                u_s = jnp.where(keep, pltpu.roll(u, d, 0), 0.0)
                u = a * u_s + u
                a = a * a_s
                d *= 2
            h = u + a * h_prev
            hgbuf[rows, ln] = h * jax.nn.gelu(ubuf[t, rows, :])
            h_prev = jnp.broadcast_to(h[SUBLANES - 1:SUBLANES, :], (SUBLANES, LANES))
        hcar[:, ln] = h_prev
    y_rnn = _dot(hgbuf[...].astype(BF16), rproj_ref[...])

    m = jnp.concatenate(
        [jax.nn.sigmoid(ubuf[n_rt + t]) * y_conv[:, t * LANES:(t + 1) * LANES]
         + jax.nn.sigmoid(ubuf[n_rt + n_mt + t]) * y_rnn[:, t * LANES:(t + 1) * LANES]
         for t in range(n_mt)], axis=1)
    mix = _dot(m.astype(BF16), wout_ref[...])
    z = alpha * x + mix
    o_ref[...] = _layer_norm(z, ln_g_ref[...], ln_b_ref[...])


def _mixer(x3d, p, layer, alpha, gate_starts):
    bsz, seq, d = x3d.shape
    tm = MIX_TM
    n_groups, conv_w, _ = p["dw_w"].shape[1:]
    d_conv = n_groups * LANES
    d_rnn = p["rc_w"].shape[-1]
    rnn_conv_w = p["rc_w"].shape[1]
    assert conv_w - 1 <= CONV_HALO and rnn_conv_w - 1 <= RNN_HALO
    n_iter, _, fill_cols = p["wf"].shape[1:]
    names = ["wc", "bc", "wf", "bf", "dw_w", "dw_b", "gn_g", "gn_b", "cproj", "rc_w", "rc_b",
             "wg", "ba", "bx", "lam", "rproj", "wout", "ln_g", "ln_b"]
    args = [p[n] for n in names]
    in_specs = [pl.BlockSpec((None, tm, d), lambda b, s: (b, s, 0))]
    in_specs += [_resident(a.shape[1:], layer) for a in args]
    kern = functools.partial(
        _mixer_kernel, alpha=alpha, tm=tm, d_conv=d_conv, d_rnn=d_rnn, d_model=d,
        conv_w=conv_w, rnn_conv_w=rnn_conv_w, gate_starts=gate_starts)
    return pl.pallas_call(
        kern,
        grid=(bsz, seq // tm),
        in_specs=in_specs,
        out_specs=pl.BlockSpec((None, tm, d), lambda b, s: (b, s, 0)),
        out_shape=jax.ShapeDtypeStruct((bsz, seq, d), F32),
        scratch_shapes=[
            pltpu.VMEM((tm, d), BF16),
            pltpu.VMEM((SUBLANES * n_groups, tm + CONV_HALO, LANES), F32),
            pltpu.VMEM((n_iter, tm, MXU_COLS), BF16),
            pltpu.VMEM((n_iter * fill_cols // LANES, tm, LANES), F32),
            pltpu.VMEM((tm + RNN_HALO, d_rnn), F32),
            pltpu.VMEM((tm, d_rnn), F32),
            pltpu.VMEM((tm, d_rnn), F32),
            pltpu.VMEM((tm, d_rnn), F32),
            pltpu.VMEM((tm, d_rnn), F32),
            pltpu.VMEM((SUBLANES, d_rnn), F32),
        ],
        compiler_params=pltpu.CompilerParams(
            dimension_semantics=("arbitrary", "arbitrary"),
            vmem_limit_bytes=VMEM_LIMIT_BYTES),
        name="mixer",
    )(x3d, *args)


def _block_diag_dense(w):
    h, bk, _ = w.shape
    eye = jnp.eye(h, dtype=w.dtype)
    return jnp.einsum("hij,hg->higj", w, eye).reshape(h * bk, h * bk)


def _gate_weights(w_a, w_x, gate_starts, k_win):
    wa = _block_diag_dense(w_a)
    wx = _block_diag_dense(w_x)
    chunks = []
    for j, ks in enumerate(gate_starts):
        cols = slice(j * LANES, (j + 1) * LANES)
        chunks.append(jnp.concatenate([wa[ks:ks + k_win, cols], wx[ks:ks + k_win, cols]], axis=1))
    return jnp.stack(chunks).astype(BF16)


def _chunk_cols(w, n_chunks):
    depth, k, n = w.shape
    width = -(-n // (n_chunks * LANES)) * LANES
    w = jnp.pad(w, ((0, 0), (0, 0), (0, n_chunks * width - n)))
    return w.reshape(depth, k, n_chunks, width).transpose(0, 2, 1, 3)


def kernel(x, ffn1_w_gu, ffn1_w_down, ln1_g, ln1_b, mix_w_in, mix_b_in, conv_dw_w, conv_dw_b, conv_gn_g, conv_gn_b, conv_w_proj, rnn_conv_w, rnn_conv_b, rnn_w_a, rnn_b_a, rnn_w_x, rnn_b_x, rnn_lambda, rnn_w_proj, mix_w_out, ln2_g, ln2_b, ffn2_w_gu, ffn2_w_down, ln3_g, ln3_b):
    bsz, seq, d = x.shape
    depth = ffn1_w_gu.shape[0]
    alpha = float((2 * depth) ** 0.25)
    n_blocks, block = rnn_w_a.shape[1], rnn_w_a.shape[2]
    gate_starts, k_win = _gate_windows(n_blocks, block)
    d_conv = conv_dw_w.shape[-1]
    d_rnn = rnn_conv_w.shape[-1]
    n_groups = d_conv // LANES
    n_iter = n_groups // GROUPS_PER_ITER
    o_fill = 2 * d_conv + d_rnn

    def row(v):
        return v[:, None, :]

    def grouped(v):
        v = v.reshape(v.shape[:-1] + (n_groups, LANES))
        return jnp.moveaxis(v, -2, 1)

    ffn1_wgu = ffn1_w_gu.astype(BF16)
    ffn1_wd = ffn1_w_down.astype(BF16)
    ffn2_wgu = ffn2_w_gu.astype(BF16)
    ffn2_wd = ffn2_w_down.astype(BF16)
    w_in = mix_w_in.astype(BF16)
    mp = {
        "wc": w_in[:, :, :o_fill], "bc": row(mix_b_in[:, :o_fill]),
        "wf": _chunk_cols(w_in[:, :, o_fill:], n_iter),
        "bf": _chunk_cols(row(mix_b_in[:, o_fill:]), n_iter),
        "dw_w": grouped(conv_dw_w), "dw_b": grouped(row(conv_dw_b)),
        "gn_g": grouped(row(conv_gn_g)), "gn_b": grouped(row(conv_gn_b)),
        "cproj": conv_w_proj.astype(BF16).reshape(depth, n_iter, MXU_COLS, d),
        "rc_w": rnn_conv_w, "rc_b": row(rnn_conv_b),
        "wg": jax.vmap(lambda a, b: _gate_weights(a, b, gate_starts, k_win))(rnn_w_a, rnn_w_x),
        "ba": row(rnn_b_a), "bx": row(rnn_b_x),
        "lam": row(rnn_lambda),
        "rproj": rnn_w_proj.astype(BF16), "wout": mix_w_out.astype(BF16),
        "ln_g": row(ln2_g), "ln_b": row(ln2_b),
    }
    ln1g, ln1b, ln3g, ln3b = row(ln1_g), row(ln1_b), row(ln3_g), row(ln3_b)

    for l in range(depth):
        h = _ffn_ln(x.reshape(bsz * seq, d), ffn1_wgu, ffn1_wd, ln1g, ln1b, l, alpha)
        h = _mixer(h.reshape(bsz, seq, d), mp, l, alpha, gate_starts)
        h = _ffn_ln(h.reshape(bsz * seq, d), ffn2_wgu, ffn2_wd, ln3g, ln3b, l, alpha)
        x = h.reshape(bsz, seq, d)
    return x
```

```python
import functools

import jax
import jax.numpy as jnp
from jax import lax
from jax.experimental import pallas as pl
from jax.experimental.pallas import tpu as pltpu

F32 = jnp.float32
BF16 = jnp.bfloat16

LN_EPS = 1e-5
RG_LRU_C = 8.0
CONV_GROUPS = 8
SUBLANES = 8
LANES = 128
MXU_COLS = 256
VMEM_LIMIT_BYTES = 56 * 1024 * 1024

FFN_TM = 512
MIX_TM = 256
CONV_ROWS = 64


def _layer_norm(z, g, b):
    mu = jnp.mean(z, axis=-1, keepdims=True)
    zc = z - mu
    var = jnp.mean(zc * zc, axis=-1, keepdims=True)
    return zc * lax.rsqrt(var + LN_EPS) * g + b


def _silu(v):
    return v * jax.nn.sigmoid(v)


def _dot(a, b):
    return jnp.dot(a, b, preferred_element_type=F32)


def _ffn_ln_kernel(x_ref, wgu_ref, wd_ref, g_ref, b_ref, o_ref, *, alpha, d_ff):
    x = x_ref[...]
    xb = x.astype(BF16)
    gate = _dot(xb, wgu_ref[:, :d_ff])
    up = _dot(xb, wgu_ref[:, d_ff:])
    act = (_silu(gate) * up).astype(BF16)
    y = _dot(act, wd_ref[...])
    z = alpha * x + 0.5 * y
    o_ref[...] = _layer_norm(z, g_ref[...], b_ref[...])


def _resident(shape, layer):
    nd = len(shape)
    return pl.BlockSpec((None,) + tuple(shape), lambda *_: (layer,) + (0,) * nd,
                        pipeline_mode=pl.Buffered(1))


def _ffn_ln(x2d, wgu, wd, g, b, layer, alpha):
    m, d = x2d.shape
    d_ff = wd.shape[1]
    tm = FFN_TM
    return pl.pallas_call(
        functools.partial(_ffn_ln_kernel, alpha=alpha, d_ff=d_ff),
        grid=(m // tm,),
        in_specs=[
            pl.BlockSpec((tm, d), lambda i: (i, 0)),
            _resident((d, 2 * d_ff), layer),
            _resident((d_ff, d), layer),
            _resident((1, d), layer),
            _resident((1, d), layer),
        ],
        out_specs=pl.BlockSpec((tm, d), lambda i: (i, 0)),
        out_shape=jax.ShapeDtypeStruct((m, d), F32),
        compiler_params=pltpu.CompilerParams(
            dimension_semantics=("arbitrary",),
            vmem_limit_bytes=VMEM_LIMIT_BYTES),
        name="ffn_ln",
    )(x2d, wgu, wd, g, b)


CONV_HALO = 32
RNN_HALO = 8


def _gate_windows(n_blocks, block):
    d = n_blocks * block
    assert d % LANES == 0
    spans = []
    for j in range(d // LANES):
        b0 = (j * LANES) // block
        b1 = (j * LANES + LANES - 1) // block
        spans.append(((b0 * block) // LANES * LANES, (b1 + 1) * block))
    k_win = min(max(-(-(e - s) // LANES) * LANES for s, e in spans), d)
    return tuple(min(s, d - k_win) for s, _ in spans), k_win


def _mixer_body(x1_ref, x2_ref, w_in_ref, b_in_ref, dw_w_ref, dw_b_ref, gn_g_ref, gn_b_ref,
                cproj_ref, rc_w_ref, rc_b_ref, wg_ref, ba_ref, bx_ref,
                lam_ref, rproj_ref, wout_ref, ln_g_ref, ln_b_ref, o_ref,
                cbuf, rbuf, ubuf, hgbuf, hcar, wr, rd,
                *, alpha, tm, d_conv, d_rnn, d_model, conv_w, rnn_conv_w, gate_starts):
    yb_w, ga_w, gx_w, rc_w = wr
    yb_r, ga_r, gx_r, rc_r = rd
    R = CONV_ROWS
    o_cg = d_conv
    o_rx = 2 * d_conv
    o_rg = o_rx + d_rnn
    n_fill = 2 * d_model + d_rnn

    def proj(xb, lo, hi):
        return _dot(xb, w_in_ref[:, lo:hi]) + b_in_ref[:, lo:hi]

    xb1 = x1_ref[...].astype(BF16)
    cbuf[CONV_HALO:CONV_HALO + tm, :] = proj(xb1, 0, o_cg) * jax.nn.sigmoid(proj(xb1, o_cg, o_rx))
    rbuf[RNN_HALO:RNN_HALO + tm, :] = proj(xb1, o_rx, o_rg)

    x2 = x2_ref[...]
    xb2 = x2.astype(BF16)
    for lo in range(0, n_fill, MXU_COLS):
        hi = min(lo + MXU_COLS, n_fill)
        ubuf[:, lo:hi] = proj(xb2, o_rg + lo, o_rg + hi)
    y_conv = _dot(yb_r[...], cproj_ref[...])

    rlead = RNN_HALO - (rnn_conv_w - 1)
    for t in range(d_rnn // LANES):
        ln = slice(t * LANES, (t + 1) * LANES)
        for r0 in range(0, tm, R):
            win = rbuf[r0:r0 + R + RNN_HALO, ln]
            acc = jnp.broadcast_to(rc_b_ref[:, ln], (R, LANES))
            for k in range(rnn_conv_w):
                acc = acc + rc_w_ref[k:k + 1, ln] * win[rlead + k:rlead + k + R]
            rc_w[r0:r0 + R, ln] = acc
    rbuf[0:RNN_HALO, :] = rbuf[tm:tm + RNN_HALO, :]
    rb = rc_w[...].astype(BF16)
    k_win = wg_ref.shape[1]
    for j, ks in enumerate(gate_starts):
        ln = slice(j * LANES, (j + 1) * LANES)
        g2 = _dot(rb[:, ks:ks + k_win], wg_ref[j])
        ga_w[:, ln] = g2[:, :LANES] + ba_ref[:, ln]
        gx_w[:, ln] = g2[:, LANES:] + bx_ref[:, ln]

    lead = CONV_HALO - (conv_w - 1)
    n_win = R + CONV_HALO
    gsz = d_conv // CONV_GROUPS

    def conv_unit(g, r0):
        ln = slice(g * gsz, (g + 1) * gsz)
        win = cbuf[r0:r0 + n_win, ln]
        acc = jnp.broadcast_to(dw_b_ref[:, ln], (R, gsz))
        for sub in range(SUBLANES):
            sh = win if sub == 0 else pltpu.roll(win, n_win - sub, 0)
            for blk in range(CONV_HALO // SUBLANES + 1):
                off = blk * SUBLANES + sub
                k = off - lead
                if 0 <= k < conv_w and off + R <= n_win:
                    acc = acc + dw_w_ref[k:k + 1, ln] * sh[blk * SUBLANES:blk * SUBLANES + R]
        mu = jnp.mean(acc, axis=-1, keepdims=True)
        xc = acc - mu
        var = jnp.mean(xc * xc, axis=-1, keepdims=True)
        yn = xc * lax.rsqrt(var + LN_EPS) * gn_g_ref[:, ln] + gn_b_ref[:, ln]
        yb_w[r0:r0 + R, ln] = _silu(yn).astype(BF16)

    conv_units = [(g, r0) for g in range(CONV_GROUPS) for r0 in range(0, tm, R)]

    neg_c_sp = -RG_LRU_C * jax.nn.softplus(-lam_ref[...])
    row = lax.broadcasted_iota(jnp.int32, (SUBLANES, LANES), 0)

    def scan_unit(t):
        ln = slice(t * LANES, (t + 1) * LANES)
        ncs = neg_c_sp[:, ln]
        h_prev = hcar[:, ln]
        for r0 in range(0, tm, SUBLANES):
            rows = slice(r0, r0 + SUBLANES)
            rgate = jax.nn.sigmoid(ga_r[rows, ln])
            igate = jax.nn.sigmoid(gx_r[rows, ln])
            log_a = rgate * ncs
            a = jnp.exp(log_a)
            w = -jnp.tanh(log_a) * (a * a + 1.0)
            mult = jnp.where(w > 0.0, w * lax.rsqrt(w), 0.0)
            u = mult * (igate * rc_r[rows, ln])
            d = 1
            while d < SUBLANES:
                keep = row >= d
                a_s = jnp.where(keep, pltpu.roll(a, d, 0), 1.0)
                u_s = jnp.where(keep, pltpu.roll(u, d, 0), 0.0)
                u = a * u_s + u
                a = a * a_s
                d *= 2
            h = u + a * h_prev
            hgbuf[rows, ln] = h * jax.nn.gelu(ubuf[rows, ln])
            h_prev = jnp.broadcast_to(h[SUBLANES - 1:SUBLANES, :], (SUBLANES, LANES))
        hcar[:, ln] = h_prev

    n_scan = d_rnn // LANES
    tail_units = len(conv_units) // 4
    per_scan = -(-(len(conv_units) - tail_units) // n_scan)
    done = 0
    for t in range(n_scan):
        scan_unit(t)
        for _ in range(per_scan):
            if done < len(conv_units) - tail_units:
                conv_unit(*conv_units[done])
                done += 1

    y_rnn = _dot(hgbuf[...].astype(BF16), rproj_ref[...])
    o_gc = d_rnn
    o_gr = d_rnn + d_model
    m = (jax.nn.sigmoid(ubuf[:, o_gc:o_gr]) * y_conv
         + jax.nn.sigmoid(ubuf[:, o_gr:o_gr + d_model]) * y_rnn)
    mix = _dot(m.astype(BF16), wout_ref[...])
    z = alpha * x2 + mix
    o_ref[...] = _layer_norm(z, ln_g_ref[...], ln_b_ref[...])

    while done < len(conv_units):
        conv_unit(*conv_units[done])
        done += 1
    cbuf[0:CONV_HALO, :] = cbuf[tm:tm + CONV_HALO, :]


def _mixer_kernel(x1_ref, x2_ref, w_in_ref, b_in_ref, dw_w_ref, dw_b_ref, gn_g_ref, gn_b_ref,
                  cproj_ref, rc_w_ref, rc_b_ref, wg_ref, ba_ref, bx_ref,
                  lam_ref, rproj_ref, wout_ref, ln_g_ref, ln_b_ref, o_ref,
                  cbuf, rbuf, ubuf, hgbuf, hcar, yb0, ga0, gx0, rc0, yb1, ga1, gx1, rc1,
                  *, alpha, tm, n_tiles, tiles_per_seq, d_conv, d_rnn, d_model,
                  conv_w, rnn_conv_w, gate_starts):
    s = pl.program_id(0)
    t1 = jnp.minimum(s, n_tiles - 1)
    t2 = jnp.maximum(s - 1, 0)

    @pl.when(s == 0)
    def _():
        for ref in (yb1, ga1, gx1, rc1):
            ref[...] = jnp.zeros(ref.shape, ref.dtype)

    @pl.when(t1 % tiles_per_seq == 0)
    def _():
        cbuf[0:CONV_HALO, :] = jnp.zeros((CONV_HALO, d_conv), F32)
        rbuf[0:RNN_HALO, :] = jnp.zeros((RNN_HALO, d_rnn), F32)

    @pl.when(t2 % tiles_per_seq == 0)
    def _():
        hcar[...] = jnp.zeros((SUBLANES, d_rnn), F32)

    def body(wr, rd):
        _mixer_body(x1_ref, x2_ref, w_in_ref, b_in_ref, dw_w_ref, dw_b_ref, gn_g_ref, gn_b_ref,
                    cproj_ref, rc_w_ref, rc_b_ref, wg_ref, ba_ref, bx_ref,
                    lam_ref, rproj_ref, wout_ref, ln_g_ref, ln_b_ref, o_ref,
                    cbuf, rbuf, ubuf, hgbuf, hcar, wr, rd,
                    alpha=alpha, tm=tm, d_conv=d_conv, d_rnn=d_rnn, d_model=d_model,
                    conv_w=conv_w, rnn_conv_w=rnn_conv_w, gate_starts=gate_starts)

    set0 = (yb0, ga0, gx0, rc0)
    set1 = (yb1, ga1, gx1, rc1)

    @pl.when(s % 2 == 0)
    def _():
        body(set0, set1)

    @pl.when(s % 2 == 1)
    def _():
        body(set1, set0)


def _mixer(x2d, p, layer, alpha, gate_starts, tiles_per_seq):
    m_rows, d = x2d.shape
    tm = MIX_TM
    n_tiles = m_rows // tm
    conv_w, d_conv = p["dw_w"].shape[1:]
    rnn_conv_w, d_rnn = p["rc_w"].shape[1:]
    assert conv_w - 1 <= CONV_HALO and rnn_conv_w - 1 <= RNN_HALO
    names = ["w_in", "b_in", "dw_w", "dw_b", "gn_g", "gn_b", "cproj", "rc_w", "rc_b",
             "wg", "ba", "bx", "lam", "rproj", "wout", "ln_g", "ln_b"]
    args = [p[n] for n in names]
    in_specs = [pl.BlockSpec((tm, d), lambda s: (jnp.minimum(s, n_tiles - 1), 0)),
                pl.BlockSpec((tm, d), lambda s: (jnp.maximum(s - 1, 0), 0))]
    in_specs += [_resident(a.shape[1:], layer) for a in args]
    kern = functools.partial(
        _mixer_kernel, alpha=alpha, tm=tm, n_tiles=n_tiles, tiles_per_seq=tiles_per_seq,
        d_conv=d_conv, d_rnn=d_rnn, d_model=d, conv_w=conv_w, rnn_conv_w=rnn_conv_w,
        gate_starts=gate_starts)
    handoff = [
        pltpu.VMEM((tm, d_conv), BF16),
        pltpu.VMEM((tm, d_rnn), F32),
        pltpu.VMEM((tm, d_rnn), F32),
        pltpu.VMEM((tm, d_rnn), F32),
    ]
    return pl.pallas_call(
        kern,
        grid=(n_tiles + 1,),
        in_specs=in_specs,
        out_specs=pl.BlockSpec((tm, d), lambda s: (jnp.maximum(s - 1, 0), 0)),
        out_shape=jax.ShapeDtypeStruct((m_rows, d), F32),
        scratch_shapes=[
            pltpu.VMEM((tm + CONV_HALO, d_conv), F32),
            pltpu.VMEM((tm + RNN_HALO, d_rnn), F32),
            pltpu.VMEM((tm, d_rnn + 2 * d), F32),
            pltpu.VMEM((tm, d_rnn), F32),
            pltpu.VMEM((SUBLANES, d_rnn), F32),
        ] + handoff + handoff,
        compiler_params=pltpu.CompilerParams(
            dimension_semantics=("arbitrary",),
            vmem_limit_bytes=VMEM_LIMIT_BYTES),
        name="mixer",
    )(x2d, x2d, *args)


def _gate_weights(w_a, w_x, gate_starts, k_win):
    n_blocks, block, _ = w_a.shape
    d = n_blocks * block
    tiling = (jnp.arange(d)[None, :] % block == jnp.arange(block)[:, None]).astype(w_a.dtype)
    same_block = (jnp.arange(d)[:, None] // block == jnp.arange(d)[None, :] // block)

    def windows(w):
        rows = w.reshape(d, block)
        out = []
        for j, ks in enumerate(gate_starts):
            cols = slice(j * LANES, (j + 1) * LANES)
            full = jnp.dot(rows[ks:ks + k_win], tiling[:, cols], precision=lax.Precision.HIGHEST)
            out.append(jnp.where(same_block[ks:ks + k_win, cols], full, 0.0))
        return jnp.stack(out)

    return jnp.concatenate([windows(w_a), windows(w_x)], axis=-1).astype(BF16)


def kernel(x, ffn1_w_gu, ffn1_w_down, ln1_g, ln1_b, mix_w_in, mix_b_in, conv_dw_w, conv_dw_b, conv_gn_g, conv_gn_b, conv_w_proj, rnn_conv_w, rnn_conv_b, rnn_w_a, rnn_b_a, rnn_w_x, rnn_b_x, rnn_lambda, rnn_w_proj, mix_w_out, ln2_g, ln2_b, ffn2_w_gu, ffn2_w_down, ln3_g, ln3_b):
    bsz, seq, d = x.shape
    depth = ffn1_w_gu.shape[0]
    alpha = float((2 * depth) ** 0.25)
    n_blocks, block = rnn_w_a.shape[1], rnn_w_a.shape[2]
    gate_starts, k_win = _gate_windows(n_blocks, block)

    def row(v):
        return v[:, None, :]

    ffn1_wgu = ffn1_w_gu.astype(BF16)
    ffn1_wd = ffn1_w_down.astype(BF16)
    ffn2_wgu = ffn2_w_gu.astype(BF16)
    ffn2_wd = ffn2_w_down.astype(BF16)
    mp = {
        "w_in": mix_w_in.astype(BF16), "b_in": row(mix_b_in),
        "dw_w": conv_dw_w, "dw_b": row(conv_dw_b),
        "gn_g": row(conv_gn_g), "gn_b": row(conv_gn_b),
        "cproj": conv_w_proj.astype(BF16),
        "rc_w": rnn_conv_w, "rc_b": row(rnn_conv_b),
        "wg": jax.vmap(lambda a, b: _gate_weights(a, b, gate_starts, k_win))(rnn_w_a, rnn_w_x),
        "ba": row(rnn_b_a), "bx": row(rnn_b_x),
        "lam": row(rnn_lambda),
        "rproj": rnn_w_proj.astype(BF16), "wout": mix_w_out.astype(BF16),
        "ln_g": row(ln2_g), "ln_b": row(ln2_b),
    }
    ln1g, ln1b, ln3g, ln3b = row(ln1_g), row(ln1_b), row(ln3_g), row(ln3_b)

    h = x.reshape(bsz * seq, d)
    for l in range(depth):
        h = _ffn_ln(h, ffn1_wgu, ffn1_wd, ln1g, ln1b, l, alpha)
        h = _mixer(h, mp, l, alpha, gate_starts, seq // MIX_TM)
        h = _ffn_ln(h, ffn2_wgu, ffn2_wd, ln3g, ln3b, l, alpha)
    return h.reshape(bsz, seq, d)
```

```python
import functools

import jax
import jax.numpy as jnp
from jax import lax
from jax.experimental import pallas as pl
from jax.experimental.pallas import tpu as pltpu

F32 = jnp.float32
BF16 = jnp.bfloat16

LN_EPS = 1e-5
RG_LRU_C = 8.0
LOG2_E = 1.4426950408889634
CONV_GROUPS = 8
SUBLANES = 8
LANES = 128
MXU_COLS = 256
VMEM_LIMIT_BYTES = 56 * 1024 * 1024

FFN_TM = 512
MIX_TM = 256
CONV_ROWS = 64


def _layer_norm(z, g, b):
    mu = jnp.mean(z, axis=-1, keepdims=True)
    zc = z - mu
    var = jnp.mean(zc * zc, axis=-1, keepdims=True)
    return zc * lax.rsqrt(var + LN_EPS) * g + b


def _silu(v):
    return v * jax.nn.sigmoid(v)


def _dot(a, b):
    return jnp.dot(a, b, preferred_element_type=F32)


def _ffn_ln_kernel(x_ref, wgu_ref, wd_ref, g_ref, b_ref, o_ref, *, alpha, d_ff):
    x = x_ref[...]
    xb = x.astype(BF16)
    gate = _dot(xb, wgu_ref[:, :d_ff])
    up = _dot(xb, wgu_ref[:, d_ff:])
    act = (_silu(gate) * up).astype(BF16)
    y = _dot(act, wd_ref[...])
    z = alpha * x + 0.5 * y
    o_ref[...] = _layer_norm(z, g_ref[...], b_ref[...])


def _resident(shape, layer):
    nd = len(shape)
    return pl.BlockSpec((None,) + tuple(shape), lambda *_: (layer,) + (0,) * nd,
                        pipeline_mode=pl.Buffered(1))


def _ffn_ln(x2d, wgu, wd, g, b, layer, alpha):
    m, d = x2d.shape
    d_ff = wd.shape[1]
    tm = FFN_TM
    return pl.pallas_call(
        functools.partial(_ffn_ln_kernel, alpha=alpha, d_ff=d_ff),
        grid=(m // tm,),
        in_specs=[
            pl.BlockSpec((tm, d), lambda i: (i, 0)),
            _resident((d, 2 * d_ff), layer),
            _resident((d_ff, d), layer),
            _resident((1, d), layer),
            _resident((1, d), layer),
        ],
        out_specs=pl.BlockSpec((tm, d), lambda i: (i, 0)),
        out_shape=jax.ShapeDtypeStruct((m, d), F32),
        compiler_params=pltpu.CompilerParams(
            dimension_semantics=("arbitrary",),
            vmem_limit_bytes=VMEM_LIMIT_BYTES),
        name="ffn_ln",
    )(x2d, wgu, wd, g, b)


CONV_HALO = 32
RNN_HALO = 8


def _gate_windows(n_blocks, block):
    d = n_blocks * block
    assert d % LANES == 0
    spans = []
    for j in range(d // LANES):
        b0 = (j * LANES) // block
        b1 = (j * LANES + LANES - 1) // block
        spans.append(((b0 * block) // LANES * LANES, (b1 + 1) * block))
    k_win = min(max(-(-(e - s) // LANES) * LANES for s, e in spans), d)
    return tuple(min(s, d - k_win) for s, _ in spans), k_win


def _mixer_body(x1_ref, x2_ref, w_in_ref, b_in_ref, dw_w_ref, dw_b_ref, gn_g_ref, gn_b_ref,
                cproj_ref, rc_w_ref, rc_b_ref, wg_ref, ba_ref, bx_ref,
                lam_ref, rproj_ref, wout_ref, ln_g_ref, ln_b_ref, o_ref,
                cbuf, rbuf, ubuf, hgbuf, hcar, wr, rd,
                *, alpha, tm, d_conv, d_rnn, d_model, conv_w, rnn_conv_w, gate_starts):
    yb_w, ga_w, gx_w, rc_w = wr
    yb_r, ga_r, gx_r, rc_r = rd
    R = CONV_ROWS
    o_cg = d_conv
    o_rx = 2 * d_conv
    o_rg = o_rx + d_rnn
    n_fill = 2 * d_model + d_rnn

    def proj(xb, lo, hi):
        return _dot(xb, w_in_ref[:, lo:hi]) + b_in_ref[:, lo:hi]

    xb1 = x1_ref[...].astype(BF16)
    cbuf[CONV_HALO:CONV_HALO + tm, :] = proj(xb1, 0, o_cg) * jax.nn.sigmoid(proj(xb1, o_cg, o_rx))
    rbuf[RNN_HALO:RNN_HALO + tm, :] = proj(xb1, o_rx, o_rg)

    x2 = x2_ref[...]
    xb2 = x2.astype(BF16)
    for lo in range(0, n_fill, MXU_COLS):
        hi = min(lo + MXU_COLS, n_fill)
        ubuf[:, lo:hi] = proj(xb2, o_rg + lo, o_rg + hi)
    y_conv = _dot(yb_r[...], cproj_ref[...])

    rlead = RNN_HALO - (rnn_conv_w - 1)
    for t in range(d_rnn // LANES):
        ln = slice(t * LANES, (t + 1) * LANES)
        for r0 in range(0, tm, R):
            win = rbuf[r0:r0 + R + RNN_HALO, ln]
            acc = jnp.broadcast_to(rc_b_ref[:, ln], (R, LANES))
            for k in range(rnn_conv_w):
                acc = acc + rc_w_ref[k:k + 1, ln] * win[rlead + k:rlead + k + R]
            rc_w[r0:r0 + R, ln] = acc
    rbuf[0:RNN_HALO, :] = rbuf[tm:tm + RNN_HALO, :]
    rb = rc_w[...].astype(BF16)
    k_win = wg_ref.shape[1]
    for j, ks in enumerate(gate_starts):
        ln = slice(j * LANES, (j + 1) * LANES)
        g2 = _dot(rb[:, ks:ks + k_win], wg_ref[j])
        ga_w[:, ln] = g2[:, :LANES] + ba_ref[:, ln]
        gx_w[:, ln] = g2[:, LANES:] + bx_ref[:, ln]

    lead = CONV_HALO - (conv_w - 1)
    n_win = R + CONV_HALO
    gsz = d_conv // CONV_GROUPS

    def conv_unit(g, r0):
        ln = slice(g * gsz, (g + 1) * gsz)
        win = cbuf[r0:r0 + n_win, ln]
        acc = jnp.broadcast_to(dw_b_ref[:, ln], (R, gsz))
        for sub in range(SUBLANES):
            sh = win if sub == 0 else pltpu.roll(win, n_win - sub, 0)
            for blk in range(CONV_HALO // SUBLANES + 1):
                off = blk * SUBLANES + sub
                k = off - lead
                if 0 <= k < conv_w and off + R <= n_win:
                    acc = acc + dw_w_ref[k:k + 1, ln] * sh[blk * SUBLANES:blk * SUBLANES + R]
        mu = jnp.mean(acc, axis=-1, keepdims=True)
        xc = acc - mu
        var = jnp.mean(xc * xc, axis=-1, keepdims=True)
        yn = xc * lax.rsqrt(var + LN_EPS) * gn_g_ref[:, ln] + gn_b_ref[:, ln]
        yb_w[r0:r0 + R, ln] = _silu(yn).astype(BF16)

    conv_units = [(g, r0) for g in range(CONV_GROUPS) for r0 in range(0, tm, R)]

    neg_c_sp = -RG_LRU_C * jax.nn.softplus(-lam_ref[...])
    neg_c_sp2 = neg_c_sp * LOG2_E
    row = lax.broadcasted_iota(jnp.int32, (SUBLANES, LANES), 0)

    def scan_unit(t):
        ln = slice(t * LANES, (t + 1) * LANES)
        ncs = neg_c_sp[:, ln]
        ncs2 = neg_c_sp2[:, ln]
        h_prev = hcar[:, ln]
        for r0 in range(0, tm, SUBLANES):
            rows = slice(r0, r0 + SUBLANES)
            rgate = jax.nn.sigmoid(ga_r[rows, ln])
            igate = jax.nn.sigmoid(gx_r[rows, ln])
            log_a = rgate * ncs
            a = jnp.exp2(rgate * ncs2)
            w = -jnp.tanh(log_a) * (a * a + 1.0)
            mult = jnp.where(w > 0.0, w * lax.rsqrt(w), 0.0)
            u = mult * (igate * rc_r[rows, ln])
            d = 1
            while d < SUBLANES:
                keep = row >= d
                a_s = jnp.where(keep, pltpu.roll(a, d, 0), 1.0)
                u_s = jnp.where(keep, pltpu.roll(u, d, 0), 0.0)
                u = a * u_s + u
                a = a * a_s
                d *= 2
            h = u + a * h_prev
            hgbuf[rows, ln] = h
            h_prev = jnp.broadcast_to(h[SUBLANES - 1:SUBLANES, :], (SUBLANES, LANES))
        hcar[:, ln] = h_prev

    n_scan = d_rnn // LANES
    tail_units = len(conv_units) // 4
    per_scan = -(-(len(conv_units) - tail_units) // n_scan)
    done = 0
    for t in range(n_scan):
        scan_unit(t)
        for _ in range(per_scan):
            if done < len(conv_units) - tail_units:
                conv_unit(*conv_units[done])
                done += 1

    y_rnn = _dot((hgbuf[...] * jax.nn.gelu(ubuf[:, 0:d_rnn])).astype(BF16), rproj_ref[...])
    o_gc = d_rnn
    o_gr = d_rnn + d_model
    m = (jax.nn.sigmoid(ubuf[:, o_gc:o_gr]) * y_conv
         + jax.nn.sigmoid(ubuf[:, o_gr:o_gr + d_model]) * y_rnn)
    mix = _dot(m.astype(BF16), wout_ref[...])
    z = alpha * x2 + mix
    o_ref[...] = _layer_norm(z, ln_g_ref[...], ln_b_ref[...])

    while done < len(conv_units):
        conv_unit(*conv_units[done])
        done += 1
    cbuf[0:CONV_HALO, :] = cbuf[tm:tm + CONV_HALO, :]


def _mixer_kernel(x1_ref, x2_ref, w_in_ref, b_in_ref, dw_w_ref, dw_b_ref, gn_g_ref, gn_b_ref,
                  cproj_ref, rc_w_ref, rc_b_ref, wg_ref, ba_ref, bx_ref,
                  lam_ref, rproj_ref, wout_ref, ln_g_ref, ln_b_ref, o_ref,
                  cbuf, rbuf, ubuf, hgbuf, hcar, yb0, ga0, gx0, rc0, yb1, ga1, gx1, rc1,
                  *, alpha, tm, n_tiles, tiles_per_seq, d_conv, d_rnn, d_model,
                  conv_w, rnn_conv_w, gate_starts):
    s = pl.program_id(0)
    t1 = jnp.minimum(s, n_tiles - 1)
    t2 = jnp.maximum(s - 1, 0)

    @pl.when(s == 0)
    def _():
        for ref in (yb1, ga1, gx1, rc1):
            ref[...] = jnp.zeros(ref.shape, ref.dtype)

    @pl.when(t1 % tiles_per_seq == 0)
    def _():
        cbuf[0:CONV_HALO, :] = jnp.zeros((CONV_HALO, d_conv), F32)
        rbuf[0:RNN_HALO, :] = jnp.zeros((RNN_HALO, d_rnn), F32)

    @pl.when(t2 % tiles_per_seq == 0)
    def _():
        hcar[...] = jnp.zeros((SUBLANES, d_rnn), F32)

    def body(wr, rd):
        _mixer_body(x1_ref, x2_ref, w_in_ref, b_in_ref, dw_w_ref, dw_b_ref, gn_g_ref, gn_b_ref,
                    cproj_ref, rc_w_ref, rc_b_ref, wg_ref, ba_ref, bx_ref,
                    lam_ref, rproj_ref, wout_ref, ln_g_ref, ln_b_ref, o_ref,
                    cbuf, rbuf, ubuf, hgbuf, hcar, wr, rd,
                    alpha=alpha, tm=tm, d_conv=d_conv, d_rnn=d_rnn, d_model=d_model,
                    conv_w=conv_w, rnn_conv_w=rnn_conv_w, gate_starts=gate_starts)

    set0 = (yb0, ga0, gx0, rc0)
    set1 = (yb1, ga1, gx1, rc1)

    @pl.when(s % 2 == 0)
    def _():
        body(set0, set1)

    @pl.when(s % 2 == 1)
    def _():
        body(set1, set0)


def _mixer(x2d, p, layer, alpha, gate_starts, tiles_per_seq):
    m_rows, d = x2d.shape
    tm = MIX_TM
    n_tiles = m_rows // tm
    conv_w, d_conv = p["dw_w"].shape[1:]
    rnn_conv_w, d_rnn = p["rc_w"].shape[1:]
    assert conv_w - 1 <= CONV_HALO and rnn_conv_w - 1 <= RNN_HALO
    names = ["w_in", "b_in", "dw_w", "dw_b", "gn_g", "gn_b", "cproj", "rc_w", "rc_b",
             "wg", "ba", "bx", "lam", "rproj", "wout", "ln_g", "ln_b"]
    args = [p[n] for n in names]
    in_specs = [pl.BlockSpec((tm, d), lambda s: (jnp.minimum(s, n_tiles - 1), 0)),
                pl.BlockSpec((tm, d), lambda s: (jnp.maximum(s - 1, 0), 0))]
    in_specs += [_resident(a.shape[1:], layer) for a in args]
    kern = functools.partial(
        _mixer_kernel, alpha=alpha, tm=tm, n_tiles=n_tiles, tiles_per_seq=tiles_per_seq,
        d_conv=d_conv, d_rnn=d_rnn, d_model=d, conv_w=conv_w, rnn_conv_w=rnn_conv_w,
        gate_starts=gate_starts)
    handoff = [
        pltpu.VMEM((tm, d_conv), BF16),
        pltpu.VMEM((tm, d_rnn), F32),
        pltpu.VMEM((tm, d_rnn), F32),
        pltpu.VMEM((tm, d_rnn), F32),
    ]
    return pl.pallas_call(
        kern,
        grid=(n_tiles + 1,),
        in_specs=in_specs,
        out_specs=pl.BlockSpec((tm, d), lambda s: (jnp.maximum(s - 1, 0), 0)),
        out_shape=jax.ShapeDtypeStruct((m_rows, d), F32),
        scratch_shapes=[
            pltpu.VMEM((tm + CONV_HALO, d_conv), F32),
            pltpu.VMEM((tm + RNN_HALO, d_rnn), F32),
            pltpu.VMEM((tm, d_rnn + 2 * d), F32),
            pltpu.VMEM((tm, d_rnn), F32),
            pltpu.VMEM((SUBLANES, d_rnn), F32),
        ] + handoff + handoff,
        compiler_params=pltpu.CompilerParams(
            dimension_semantics=("arbitrary",),
            vmem_limit_bytes=VMEM_LIMIT_BYTES),
        name="mixer",
    )(x2d, x2d, *args)


def _gate_weights(w_a, w_x, gate_starts, k_win):
    n_blocks, block, _ = w_a.shape
    d = n_blocks * block
    tiling = (jnp.arange(d)[None, :] % block == jnp.arange(block)[:, None]).astype(w_a.dtype)
    same_block = (jnp.arange(d)[:, None] // block == jnp.arange(d)[None, :] // block)

    def windows(w):
        rows = w.reshape(d, block)
        out = []
        for j, ks in enumerate(gate_starts):
            cols = slice(j * LANES, (j + 1) * LANES)
            full = jnp.dot(rows[ks:ks + k_win], tiling[:, cols], precision=lax.Precision.HIGHEST)
            out.append(jnp.where(same_block[ks:ks + k_win, cols], full, 0.0))
        return jnp.stack(out)

    return jnp.concatenate([windows(w_a), windows(w_x)], axis=-1).astype(BF16)


def kernel(x, ffn1_w_gu, ffn1_w_down, ln1_g, ln1_b, mix_w_in, mix_b_in, conv_dw_w, conv_dw_b, conv_gn_g, conv_gn_b, conv_w_proj, rnn_conv_w, rnn_conv_b, rnn_w_a, rnn_b_a, rnn_w_x, rnn_b_x, rnn_lambda, rnn_w_proj, mix_w_out, ln2_g, ln2_b, ffn2_w_gu, ffn2_w_down, ln3_g, ln3_b):
    bsz, seq, d = x.shape
    depth = ffn1_w_gu.shape[0]
    alpha = float((2 * depth) ** 0.25)
    n_blocks, block = rnn_w_a.shape[1], rnn_w_a.shape[2]
    gate_starts, k_win = _gate_windows(n_blocks, block)

    def row(v):
        return v[:, None, :]

    ffn1_wgu = ffn1_w_gu.astype(BF16)
    ffn1_wd = ffn1_w_down.astype(BF16)
    ffn2_wgu = ffn2_w_gu.astype(BF16)
    ffn2_wd = ffn2_w_down.astype(BF16)
    mp = {
        "w_in": mix_w_in.astype(BF16), "b_in": row(mix_b_in),
        "dw_w": conv_dw_w, "dw_b": row(conv_dw_b),
        "gn_g": row(conv_gn_g), "gn_b": row(conv_gn_b),
        "cproj": conv_w_proj.astype(BF16),
        "rc_w": rnn_conv_w, "rc_b": row(rnn_conv_b),
        "wg": jax.vmap(lambda a, b: _gate_weights(a, b, gate_starts, k_win))(rnn_w_a, rnn_w_x),
        "ba": row(rnn_b_a), "bx": row(rnn_b_x),
        "lam": row(rnn_lambda),
        "rproj": rnn_w_proj.astype(BF16), "wout": mix_w_out.astype(BF16),
        "ln_g": row(ln2_g), "ln_b": row(ln2_b),
    }
    ln1g, ln1b, ln3g, ln3b = row(ln1_g), row(ln1_b), row(ln3_g), row(ln3_b)

    h = x.reshape(bsz * seq, d)
    for l in range(depth):
        h = _ffn_ln(h, ffn1_wgu, ffn1_wd, ln1g, ln1b, l, alpha)
        h = _mixer(h, mp, l, alpha, gate_starts, seq // MIX_TM)
        h = _ffn_ln(h, ffn2_wgu, ffn2_wd, ln3g, ln3b, l, alpha)
    return h.reshape(bsz, seq, d)
```

```python
import functools

import jax
import jax.numpy as jnp
from jax import lax
from jax.experimental import pallas as pl
from jax.experimental.pallas import tpu as pltpu

F32 = jnp.float32
BF16 = jnp.bfloat16

LN_EPS = 1e-5
RG_LRU_C = 8.0
LOG2_E = 1.4426950408889634
CONV_GROUPS = 8
SUBLANES = 8
LANES = 128
MXU_COLS = 256
VMEM_LIMIT_BYTES = 56 * 1024 * 1024

FFN_TM = 512
MIX_TM = 256
CONV_ROWS = 64


def _layer_norm(z, g, b):
    mu = jnp.mean(z, axis=-1, keepdims=True)
    zc = z - mu
    var = jnp.mean(zc * zc, axis=-1, keepdims=True)
    return zc * lax.rsqrt(var + LN_EPS) * g + b


def _silu(v):
    return v * jax.nn.sigmoid(v)


def _dot(a, b):
    return jnp.dot(a, b, preferred_element_type=F32)


def _ffn_ln_kernel(*refs, alpha, d_ff, n_side):
    x_ref, wgu_ref, wd_ref, g_ref, b_ref = refs[:5]
    side_in = refs[5:5 + n_side]
    o_ref = refs[5 + n_side]
    side_out = refs[6 + n_side:6 + 2 * n_side]
    x = x_ref[...]
    xb = x.astype(BF16)
    gate = _dot(xb, wgu_ref[:, :d_ff])
    up = _dot(xb, wgu_ref[:, d_ff:])
    act = (_silu(gate) * up).astype(BF16)
    y = _dot(act, wd_ref[...])
    z = alpha * x + 0.5 * y
    o_ref[...] = _layer_norm(z, g_ref[...], b_ref[...])
    for src, dst in zip(side_in, side_out):
        dst[...] = src[...].astype(BF16)


def _resident(shape, layer=None):
    nd = len(shape)
    if layer is None:
        return pl.BlockSpec(tuple(shape), lambda *_: (0,) * nd, pipeline_mode=pl.Buffered(1))
    return pl.BlockSpec((None,) + tuple(shape), lambda *_: (layer,) + (0,) * nd,
                        pipeline_mode=pl.Buffered(1))


def _ffn_ln(x2d, wgu, wd, g, b, layer, alpha, side=()):
    m, d = x2d.shape
    d_ff = wd.shape[1]
    tm = FFN_TM
    n_steps = m // tm
    side_specs_in, side_specs_out, side_shapes = [], [], []
    for w, wl in side:
        rows, cols = w.shape[1:]
        slab = rows // n_steps
        assert slab * n_steps == rows and slab % (2 * SUBLANES) == 0
        side_specs_in.append(pl.BlockSpec((None, slab, cols), lambda i, wl=wl: (wl, i, 0)))
        side_specs_out.append(pl.BlockSpec((slab, cols), lambda i: (i, 0)))
        side_shapes.append(jax.ShapeDtypeStruct((rows, cols), BF16))
    outs = pl.pallas_call(
        functools.partial(_ffn_ln_kernel, alpha=alpha, d_ff=d_ff, n_side=len(side)),
        grid=(n_steps,),
        in_specs=[
            pl.BlockSpec((tm, d), lambda i: (i, 0)),
            _resident((d, 2 * d_ff)),
            _resident((d_ff, d), layer),
            _resident((1, d), layer),
            _resident((1, d), layer),
        ] + side_specs_in,
        out_specs=[pl.BlockSpec((tm, d), lambda i: (i, 0))] + side_specs_out,
        out_shape=[jax.ShapeDtypeStruct((m, d), F32)] + side_shapes,
        compiler_params=pltpu.CompilerParams(
            dimension_semantics=("arbitrary",),
            vmem_limit_bytes=VMEM_LIMIT_BYTES),
        name="ffn_ln",
    )(x2d, wgu, wd, g, b, *[w for w, _ in side])
    return outs[0], outs[1:]


CONV_HALO = 32
RNN_HALO = 8


def _gate_windows(n_blocks, block):
    d = n_blocks * block
    assert d % LANES == 0
    spans = []
    for j in range(d // LANES):
        b0 = (j * LANES) // block
        b1 = (j * LANES + LANES - 1) // block
        spans.append(((b0 * block) // LANES * LANES, (b1 + 1) * block))
    k_win = min(max(-(-(e - s) // LANES) * LANES for s, e in spans), d)
    return tuple(min(s, d - k_win) for s, _ in spans), k_win


def _mixer_body(x1_ref, x2_ref, w_in_ref, b_in_ref, dw_w_ref, dw_b_ref, gn_g_ref, gn_b_ref,
                cproj_ref, rc_w_ref, rc_b_ref, wg_ref, ba_ref, bx_ref,
                lam_ref, rproj_ref, wout_ref, ln_g_ref, ln_b_ref, o_ref,
                cbuf, rbuf, ubuf, hgbuf, hcar, wr, rd,
                *, alpha, tm, d_conv, d_rnn, d_model, conv_w, rnn_conv_w, gate_starts):
    yb_w, ga_w, gx_w, rc_w = wr
    yb_r, ga_r, gx_r, rc_r = rd
    R = CONV_ROWS
    o_cg = d_conv
    o_rx = 2 * d_conv
    o_rg = o_rx + d_rnn
    n_fill = 2 * d_model + d_rnn

    def proj(xb, lo, hi):
        return _dot(xb, w_in_ref[:, lo:hi]) + b_in_ref[:, lo:hi]

    xb1 = x1_ref[...].astype(BF16)
    cbuf[CONV_HALO:CONV_HALO + tm, :] = proj(xb1, 0, o_cg) * jax.nn.sigmoid(proj(xb1, o_cg, o_rx))
    rbuf[RNN_HALO:RNN_HALO + tm, :] = proj(xb1, o_rx, o_rg)

    x2 = x2_ref[...]
    xb2 = x2.astype(BF16)
    for lo in range(0, n_fill, MXU_COLS):
        hi = min(lo + MXU_COLS, n_fill)
        ubuf[:, lo:hi] = proj(xb2, o_rg + lo, o_rg + hi)
    y_conv = _dot(yb_r[...], cproj_ref[...])

    rlead = RNN_HALO - (rnn_conv_w - 1)
    for t in range(d_rnn // LANES):
        ln = slice(t * LANES, (t + 1) * LANES)
        for r0 in range(0, tm, R):
            win = rbuf[r0:r0 + R + RNN_HALO, ln]
            acc = jnp.broadcast_to(rc_b_ref[:, ln], (R, LANES))
            for k in range(rnn_conv_w):
                acc = acc + rc_w_ref[k:k + 1, ln] * win[rlead + k:rlead + k + R]
            rc_w[r0:r0 + R, ln] = acc
    rbuf[0:RNN_HALO, :] = rbuf[tm:tm + RNN_HALO, :]
    rb = rc_w[...].astype(BF16)
    k_win = wg_ref.shape[1]
    for j, ks in enumerate(gate_starts):
        ln = slice(j * LANES, (j + 1) * LANES)
        g2 = _dot(rb[:, ks:ks + k_win], wg_ref[j])
        ga_w[:, ln] = g2[:, :LANES] + ba_ref[:, ln]
        gx_w[:, ln] = g2[:, LANES:] + bx_ref[:, ln]

    lead = CONV_HALO - (conv_w - 1)
    n_win = R + CONV_HALO
    gsz = d_conv // CONV_GROUPS

    def conv_unit(g, r0):
        ln = slice(g * gsz, (g + 1) * gsz)
        win = cbuf[r0:r0 + n_win, ln]
        acc = jnp.broadcast_to(dw_b_ref[:, ln], (R, gsz))
        for sub in range(SUBLANES):
            sh = win if sub == 0 else pltpu.roll(win, n_win - sub, 0)
            for blk in range(CONV_HALO // SUBLANES + 1):
                off = blk * SUBLANES + sub
                k = off - lead
                if 0 <= k < conv_w and off + R <= n_win:
                    acc = acc + dw_w_ref[k:k + 1, ln] * sh[blk * SUBLANES:blk * SUBLANES + R]
        mu = jnp.mean(acc, axis=-1, keepdims=True)
        xc = acc - mu
        var = jnp.mean(xc * xc, axis=-1, keepdims=True)
        yn = xc * lax.rsqrt(var + LN_EPS) * gn_g_ref[:, ln] + gn_b_ref[:, ln]
        yb_w[r0:r0 + R, ln] = _silu(yn).astype(BF16)

    conv_units = [(g, r0) for g in range(CONV_GROUPS) for r0 in range(0, tm, R)]

    neg_c_sp = -RG_LRU_C * jax.nn.softplus(-lam_ref[...])
    neg_c_sp2 = neg_c_sp * LOG2_E
    row = lax.broadcasted_iota(jnp.int32, (SUBLANES, LANES), 0)

    def scan_unit(t):
        ln = slice(t * LANES, (t + 1) * LANES)
        ncs = neg_c_sp[:, ln]
        ncs2 = neg_c_sp2[:, ln]
        h_prev = hcar[:, ln]
        for r0 in range(0, tm, SUBLANES):
            rows = slice(r0, r0 + SUBLANES)
            rgate = jax.nn.sigmoid(ga_r[rows, ln])
            igate = jax.nn.sigmoid(gx_r[rows, ln])
            log_a = rgate * ncs
            a = jnp.exp2(rgate * ncs2)
            w = -jnp.tanh(log_a) * (a * a + 1.0)
            mult = jnp.where(w > 0.0, w * lax.rsqrt(w), 0.0)
            u = mult * (igate * rc_r[rows, ln])
            d = 1
            while d < SUBLANES:
                keep = row >= d
                a_s = jnp.where(keep, pltpu.roll(a, d, 0), 1.0)
                u_s = jnp.where(keep, pltpu.roll(u, d, 0), 0.0)
                u = a * u_s + u
                a = a * a_s
                d *= 2
            h = u + a * h_prev
            hgbuf[rows, ln] = h
            h_prev = jnp.broadcast_to(h[SUBLANES - 1:SUBLANES, :], (SUBLANES, LANES))
        hcar[:, ln] = h_prev

    n_scan = d_rnn // LANES
    tail_units = len(conv_units) // 4
    per_scan = -(-(len(conv_units) - tail_units) // n_scan)
    done = 0
    for t in range(n_scan):
        scan_unit(t)
        for _ in range(per_scan):
            if done < len(conv_units) - tail_units:
                conv_unit(*conv_units[done])
                done += 1

    y_rnn = _dot((hgbuf[...] * jax.nn.gelu(ubuf[:, 0:d_rnn])).astype(BF16), rproj_ref[...])
    o_gc = d_rnn
    o_gr = d_rnn + d_model
    m = (jax.nn.sigmoid(ubuf[:, o_gc:o_gr]) * y_conv
         + jax.nn.sigmoid(ubuf[:, o_gr:o_gr + d_model]) * y_rnn)
    mix = _dot(m.astype(BF16), wout_ref[...])
    z = alpha * x2 + mix
    o_ref[...] = _layer_norm(z, ln_g_ref[...], ln_b_ref[...])

    while done < len(conv_units):
        conv_unit(*conv_units[done])
        done += 1
    cbuf[0:CONV_HALO, :] = cbuf[tm:tm + CONV_HALO, :]


def _mixer_kernel(x1_ref, x2_ref, w_in_ref, b_in_ref, dw_w_ref, dw_b_ref, gn_g_ref, gn_b_ref,
                  cproj_ref, rc_w_ref, rc_b_ref, wg_ref, ba_ref, bx_ref,
                  lam_ref, rproj_ref, wout_ref, ln_g_ref, ln_b_ref, o_ref,
                  cbuf, rbuf, ubuf, hgbuf, hcar, yb0, ga0, gx0, rc0, yb1, ga1, gx1, rc1,
                  *, alpha, tm, n_tiles, tiles_per_seq, d_conv, d_rnn, d_model,
                  conv_w, rnn_conv_w, gate_starts):
    s = pl.program_id(0)
    t1 = jnp.minimum(s, n_tiles - 1)
    t2 = jnp.maximum(s - 1, 0)

    @pl.when(s == 0)
    def _():
        for ref in (yb1, ga1, gx1, rc1):
            ref[...] = jnp.zeros(ref.shape, ref.dtype)

    @pl.when(t1 % tiles_per_seq == 0)
    def _():
        cbuf[0:CONV_HALO, :] = jnp.zeros((CONV_HALO, d_conv), F32)
        rbuf[0:RNN_HALO, :] = jnp.zeros((RNN_HALO, d_rnn), F32)

    @pl.when(t2 % tiles_per_seq == 0)
    def _():
        hcar[...] = jnp.zeros((SUBLANES, d_rnn), F32)

    def body(wr, rd):
        _mixer_body(x1_ref, x2_ref, w_in_ref, b_in_ref, dw_w_ref, dw_b_ref, gn_g_ref, gn_b_ref,
                    cproj_ref, rc_w_ref, rc_b_ref, wg_ref, ba_ref, bx_ref,
                    lam_ref, rproj_ref, wout_ref, ln_g_ref, ln_b_ref, o_ref,
                    cbuf, rbuf, ubuf, hgbuf, hcar, wr, rd,
                    alpha=alpha, tm=tm, d_conv=d_conv, d_rnn=d_rnn, d_model=d_model,
                    conv_w=conv_w, rnn_conv_w=rnn_conv_w, gate_starts=gate_starts)

    set0 = (yb0, ga0, gx0, rc0)
    set1 = (yb1, ga1, gx1, rc1)

    @pl.when(s % 2 == 0)
    def _():
        body(set0, set1)

    @pl.when(s % 2 == 1)
    def _():
        body(set1, set0)


def _mixer(x2d, w_in, p, layer, alpha, gate_starts, tiles_per_seq):
    m_rows, d = x2d.shape
    tm = MIX_TM
    n_tiles = m_rows // tm
    conv_w, d_conv = p["dw_w"].shape[1:]
    rnn_conv_w, d_rnn = p["rc_w"].shape[1:]
    assert conv_w - 1 <= CONV_HALO and rnn_conv_w - 1 <= RNN_HALO
    names = ["w_in", "b_in", "dw_w", "dw_b", "gn_g", "gn_b", "cproj", "rc_w", "rc_b",
             "wg", "ba", "bx", "lam", "rproj", "wout", "ln_g", "ln_b"]
    args = [w_in] + [p[n] for n in names[1:]]
    in_specs = [pl.BlockSpec((tm, d), lambda s: (jnp.minimum(s, n_tiles - 1), 0)),
                pl.BlockSpec((tm, d), lambda s: (jnp.maximum(s - 1, 0), 0)),
                _resident(w_in.shape)]
    in_specs += [_resident(a.shape[1:], layer) for a in args[1:]]
    kern = functools.partial(
        _mixer_kernel, alpha=alpha, tm=tm, n_tiles=n_tiles, tiles_per_seq=tiles_per_seq,
        d_conv=d_conv, d_rnn=d_rnn, d_model=d, conv_w=conv_w, rnn_conv_w=rnn_conv_w,
        gate_starts=gate_starts)
    handoff = [
        pltpu.VMEM((tm, d_conv), BF16),
        pltpu.VMEM((tm, d_rnn), F32),
        pltpu.VMEM((tm, d_rnn), F32),
        pltpu.VMEM((tm, d_rnn), F32),
    ]
    return pl.pallas_call(
        kern,
        grid=(n_tiles + 1,),
        in_specs=in_specs,
        out_specs=pl.BlockSpec((tm, d), lambda s: (jnp.maximum(s - 1, 0), 0)),
        out_shape=jax.ShapeDtypeStruct((m_rows, d), F32),
        scratch_shapes=[
            pltpu.VMEM((tm + CONV_HALO, d_conv), F32),
            pltpu.VMEM((tm + RNN_HALO, d_rnn), F32),
            pltpu.VMEM((tm, d_rnn + 2 * d), F32),
            pltpu.VMEM((tm, d_rnn), F32),
            pltpu.VMEM((SUBLANES, d_rnn), F32),
        ] + handoff + handoff,
        compiler_params=pltpu.CompilerParams(
            dimension_semantics=("arbitrary",),
            vmem_limit_bytes=VMEM_LIMIT_BYTES),
        name="mixer",
    )(x2d, x2d, *args)


def _gate_weights(w_a, w_x, gate_starts, k_win):
    n_blocks, block, _ = w_a.shape
    d = n_blocks * block
    tiling = (jnp.arange(d)[None, :] % block == jnp.arange(block)[:, None]).astype(w_a.dtype)
    same_block = (jnp.arange(d)[:, None] // block == jnp.arange(d)[None, :] // block)

    def windows(w):
        rows = w.reshape(d, block)
        out = []
        for j, ks in enumerate(gate_starts):
            cols = slice(j * LANES, (j + 1) * LANES)
            full = jnp.dot(rows[ks:ks + k_win], tiling[:, cols], precision=lax.Precision.HIGHEST)
            out.append(jnp.where(same_block[ks:ks + k_win, cols], full, 0.0))
        return jnp.stack(out)

    return jnp.concatenate([windows(w_a), windows(w_x)], axis=-1).astype(BF16)


def kernel(x, ffn1_w_gu, ffn1_w_down, ln1_g, ln1_b, mix_w_in, mix_b_in, conv_dw_w, conv_dw_b, conv_gn_g, conv_gn_b, conv_w_proj, rnn_conv_w, rnn_conv_b, rnn_w_a, rnn_b_a, rnn_w_x, rnn_b_x, rnn_lambda, rnn_w_proj, mix_w_out, ln2_g, ln2_b, ffn2_w_gu, ffn2_w_down, ln3_g, ln3_b):
    bsz, seq, d = x.shape
    depth = ffn1_w_gu.shape[0]
    alpha = float((2 * depth) ** 0.25)
    n_blocks, block = rnn_w_a.shape[1], rnn_w_a.shape[2]
    gate_starts, k_win = _gate_windows(n_blocks, block)

    def row(v):
        return v[:, None, :]

    ffn1_wd = ffn1_w_down.astype(BF16)
    ffn2_wd = ffn2_w_down.astype(BF16)
    mp = {
        "b_in": row(mix_b_in),
        "dw_w": conv_dw_w, "dw_b": row(conv_dw_b),
        "gn_g": row(conv_gn_g), "gn_b": row(conv_gn_b),
        "cproj": conv_w_proj.astype(BF16),
        "rc_w": rnn_conv_w, "rc_b": row(rnn_conv_b),
        "wg": jax.vmap(lambda a, b: _gate_weights(a, b, gate_starts, k_win))(rnn_w_a, rnn_w_x),
        "ba": row(rnn_b_a), "bx": row(rnn_b_x),
        "lam": row(rnn_lambda),
        "rproj": rnn_w_proj.astype(BF16), "wout": mix_w_out.astype(BF16),
        "ln_g": row(ln2_g), "ln_b": row(ln2_b),
    }
    ln1g, ln1b, ln3g, ln3b = row(ln1_g), row(ln1_b), row(ln3_g), row(ln3_b)

    wgu1 = ffn1_w_gu[0].astype(BF16)
    h = x.reshape(bsz * seq, d)
    for l in range(depth):
        h, (w_in, wgu2) = _ffn_ln(h, wgu1, ffn1_wd, ln1g, ln1b, l, alpha,
                                  side=((mix_w_in, l), (ffn2_w_gu, l)))
        h = _mixer(h, w_in, mp, l, alpha, gate_starts, seq // MIX_TM)
        nxt = ((ffn1_w_gu, l + 1),) if l + 1 < depth else ()
        h, cast = _ffn_ln(h, wgu2, ffn2_wd, ln3g, ln3b, l, alpha, side=nxt)
        if cast:
            wgu1 = cast[0]
    return h.reshape(bsz, seq, d)
```

```python
import functools

import jax
import jax.numpy as jnp
from jax import lax
from jax.experimental import pallas as pl
from jax.experimental.pallas import tpu as pltpu

F32 = jnp.float32
BF16 = jnp.bfloat16

LN_EPS = 1e-5
RG_LRU_C = 8.0
LOG2_E = 1.4426950408889634
CONV_GROUPS = 8
SUBLANES = 8
LANES = 128
MXU_COLS = 256
VMEM_LIMIT_BYTES = 56 * 1024 * 1024

FFN_TM = 512
MIX_TM = 256
CONV_ROWS = 64


def _layer_norm(z, g, b):
    mu = jnp.mean(z, axis=-1, keepdims=True)
    zc = z - mu
    var = jnp.mean(zc * zc, axis=-1, keepdims=True)
    return zc * lax.rsqrt(var + LN_EPS) * g + b


def _silu(v):
    return v * jax.nn.sigmoid(v)


def _dot(a, b):
    return jnp.dot(a, b, preferred_element_type=F32)


def _ffn_ln_kernel(*refs, alpha, d_ff, n_side):
    x_ref, wgu_ref, wd_ref, g_ref, b_ref = refs[:5]
    side_in = refs[5:5 + n_side]
    o_ref = refs[5 + n_side]
    side_out = refs[6 + n_side:6 + 2 * n_side]
    x = x_ref[...]
    xb = x.astype(BF16)
    gate = _dot(xb, wgu_ref[:, :d_ff])
    up = _dot(xb, wgu_ref[:, d_ff:])
    act = (_silu(gate) * up).astype(BF16)
    half = x.shape[0] // 2
    for r0 in (0, half):
        y = _dot(act[r0:r0 + half], wd_ref[...])
        z = alpha * x[r0:r0 + half] + 0.5 * y
        o_ref[r0:r0 + half, :] = _layer_norm(z, g_ref[...], b_ref[...])
    for src, dst in zip(side_in, side_out):
        dst[...] = src[...].astype(BF16)


def _resident(shape, layer=None):
    nd = len(shape)
    if layer is None:
        return pl.BlockSpec(tuple(shape), lambda *_: (0,) * nd, pipeline_mode=pl.Buffered(1))
    return pl.BlockSpec((None,) + tuple(shape), lambda *_: (layer,) + (0,) * nd,
                        pipeline_mode=pl.Buffered(1))


def _ffn_ln(x2d, wgu, wd, g, b, layer, alpha, side=()):
    m, d = x2d.shape
    d_ff = wd.shape[1]
    tm = FFN_TM
    n_steps = m // tm
    side_specs_in, side_specs_out, side_shapes = [], [], []
    for w, wl in side:
        rows, cols = w.shape[1:]
        slab = rows // n_steps
        assert slab * n_steps == rows and slab % (2 * SUBLANES) == 0
        side_specs_in.append(pl.BlockSpec((None, slab, cols), lambda i, wl=wl: (wl, i, 0)))
        side_specs_out.append(pl.BlockSpec((slab, cols), lambda i: (i, 0)))
        side_shapes.append(jax.ShapeDtypeStruct((rows, cols), BF16))
    outs = pl.pallas_call(
        functools.partial(_ffn_ln_kernel, alpha=alpha, d_ff=d_ff, n_side=len(side)),
        grid=(n_steps,),
        in_specs=[
            pl.BlockSpec((tm, d), lambda i: (i, 0)),
            _resident((d, 2 * d_ff)),
            _resident((d_ff, d), layer),
            _resident((1, d), layer),
            _resident((1, d), layer),
        ] + side_specs_in,
        out_specs=[pl.BlockSpec((tm, d), lambda i: (i, 0))] + side_specs_out,
        out_shape=[jax.ShapeDtypeStruct((m, d), F32)] + side_shapes,
        compiler_params=pltpu.CompilerParams(
            dimension_semantics=("arbitrary",),
            vmem_limit_bytes=VMEM_LIMIT_BYTES),
        name="ffn_ln",
    )(x2d, wgu, wd, g, b, *[w for w, _ in side])
    return outs[0], outs[1:]


CONV_HALO = 32
RNN_HALO = 8


def _gate_windows(n_blocks, block):
    d = n_blocks * block
    assert d % LANES == 0
    spans = []
    for j in range(d // LANES):
        b0 = (j * LANES) // block
        b1 = (j * LANES + LANES - 1) // block
        spans.append(((b0 * block) // LANES * LANES, (b1 + 1) * block))
    k_win = min(max(-(-(e - s) // LANES) * LANES for s, e in spans), d)
    return tuple(min(s, d - k_win) for s, _ in spans), k_win


def _mixer_body(x1_ref, x2_ref, w_in_ref, b_in_ref, dw_w_ref, dw_b_ref, gn_g_ref, gn_b_ref,
                cproj_ref, rc_w_ref, rc_b_ref, wg_ref, ba_ref, bx_ref,
                lam_ref, rproj_ref, wout_ref, ln_g_ref, ln_b_ref, o_ref,
                cbuf, rbuf, ubuf, hgbuf, hcar, wr, rd,
                *, alpha, tm, d_conv, d_rnn, d_model, conv_w, rnn_conv_w, gate_starts):
    yb_w, ga_w, gx_w, rc_w = wr
    yb_r, ga_r, gx_r, rc_r = rd
    R = CONV_ROWS
    o_cg = d_conv
    o_rx = 2 * d_conv
    o_rg = o_rx + d_rnn
    n_fill = 2 * d_model + d_rnn

    def proj(xb, lo, hi):
        return _dot(xb, w_in_ref[:, lo:hi]) + b_in_ref[:, lo:hi]

    xb1 = x1_ref[...].astype(BF16)
    cbuf[CONV_HALO:CONV_HALO + tm, :] = proj(xb1, 0, o_cg) * jax.nn.sigmoid(proj(xb1, o_cg, o_rx))
    rbuf[RNN_HALO:RNN_HALO + tm, :] = proj(xb1, o_rx, o_rg)

    x2 = x2_ref[...]
    xb2 = x2.astype(BF16)
    for lo in range(0, n_fill, MXU_COLS):
        hi = min(lo + MXU_COLS, n_fill)
        ubuf[:, lo:hi] = proj(xb2, o_rg + lo, o_rg + hi)
    y_conv = _dot(yb_r[...], cproj_ref[...])

    rlead = RNN_HALO - (rnn_conv_w - 1)
    for t in range(d_rnn // LANES):
        ln = slice(t * LANES, (t + 1) * LANES)
        for r0 in range(0, tm, R):
            win = rbuf[r0:r0 + R + RNN_HALO, ln]
            acc = jnp.broadcast_to(rc_b_ref[:, ln], (R, LANES))
            for k in range(rnn_conv_w):
                acc = acc + rc_w_ref[k:k + 1, ln] * win[rlead + k:rlead + k + R]
            rc_w[r0:r0 + R, ln] = acc
    rbuf[0:RNN_HALO, :] = rbuf[tm:tm + RNN_HALO, :]
    rb = rc_w[...].astype(BF16)
    k_win = wg_ref.shape[1]
    for j, ks in enumerate(gate_starts):
        ln = slice(j * LANES, (j + 1) * LANES)
        g2 = _dot(rb[:, ks:ks + k_win], wg_ref[j])
        ga_w[:, ln] = g2[:, :LANES] + ba_ref[:, ln]
        gx_w[:, ln] = g2[:, LANES:] + bx_ref[:, ln]

    lead = CONV_HALO - (conv_w - 1)
    n_win = R + CONV_HALO
    gsz = d_conv // CONV_GROUPS

    def conv_unit(g, r0):
        ln = slice(g * gsz, (g + 1) * gsz)
        win = cbuf[r0:r0 + n_win, ln]
        acc = jnp.broadcast_to(dw_b_ref[:, ln], (R, gsz))
        for sub in range(SUBLANES):
            sh = win if sub == 0 else pltpu.roll(win, n_win - sub, 0)
            for blk in range(CONV_HALO // SUBLANES + 1):
                off = blk * SUBLANES + sub
                k = off - lead
                if 0 <= k < conv_w and off + R <= n_win:
                    acc = acc + dw_w_ref[k:k + 1, ln] * sh[blk * SUBLANES:blk * SUBLANES + R]
        mu = jnp.mean(acc, axis=-1, keepdims=True)
        xc = acc - mu
        var = jnp.mean(xc * xc, axis=-1, keepdims=True)
        yn = xc * lax.rsqrt(var + LN_EPS) * gn_g_ref[:, ln] + gn_b_ref[:, ln]
        yb_w[r0:r0 + R, ln] = _silu(yn).astype(BF16)

    conv_units = [(g, r0) for g in range(CONV_GROUPS) for r0 in range(0, tm, R)]

    neg_c_sp = -RG_LRU_C * jax.nn.softplus(-lam_ref[...])
    neg_c_sp2 = neg_c_sp * LOG2_E
    row = lax.broadcasted_iota(jnp.int32, (SUBLANES, LANES), 0)

    def scan_unit(t):
        ln = slice(t * LANES, (t + 1) * LANES)
        ncs = neg_c_sp[:, ln]
        ncs2 = neg_c_sp2[:, ln]
        h_prev = hcar[:, ln]
        for r0 in range(0, tm, SUBLANES):
            rows = slice(r0, r0 + SUBLANES)
            rgate = jax.nn.sigmoid(ga_r[rows, ln])
            igate = jax.nn.sigmoid(gx_r[rows, ln])
            log_a = rgate * ncs
            a = jnp.exp2(rgate * ncs2)
            w = -jnp.tanh(log_a) * (a * a + 1.0)
            mult = jnp.where(w > 0.0, w * lax.rsqrt(w), 0.0)
            u = mult * (igate * rc_r[rows, ln])
            d = 1
            while d < SUBLANES:
                keep = row >= d
                a_s = jnp.where(keep, pltpu.roll(a, d, 0), 1.0)
                u_s = jnp.where(keep, pltpu.roll(u, d, 0), 0.0)
                u = a * u_s + u
                a = a * a_s
                d *= 2
            h = u + a * h_prev
            hgbuf[rows, ln] = h
            h_prev = jnp.broadcast_to(h[SUBLANES - 1:SUBLANES, :], (SUBLANES, LANES))
        hcar[:, ln] = h_prev

    n_scan = d_rnn // LANES
    tail_units = len(conv_units) // 4
    per_scan = -(-(len(conv_units) - tail_units) // n_scan)
    done = 0
    for t in range(n_scan):
        scan_unit(t)
        for _ in range(per_scan):
            if done < len(conv_units) - tail_units:
                conv_unit(*conv_units[done])
                done += 1

    y_rnn = _dot((hgbuf[...] * jax.nn.gelu(ubuf[:, 0:d_rnn])).astype(BF16), rproj_ref[...])
    o_gc = d_rnn
    o_gr = d_rnn + d_model
    m = (jax.nn.sigmoid(ubuf[:, o_gc:o_gr]) * y_conv
         + jax.nn.sigmoid(ubuf[:, o_gr:o_gr + d_model]) * y_rnn)
    mix = _dot(m.astype(BF16), wout_ref[...])
    z = alpha * x2 + mix
    o_ref[...] = _layer_norm(z, ln_g_ref[...], ln_b_ref[...])

    while done < len(conv_units):
        conv_unit(*conv_units[done])
        done += 1
    cbuf[0:CONV_HALO, :] = cbuf[tm:tm + CONV_HALO, :]


def _mixer_kernel(x1_ref, x2_ref, w_in_ref, b_in_ref, dw_w_ref, dw_b_ref, gn_g_ref, gn_b_ref,
                  cproj_ref, rc_w_ref, rc_b_ref, wg_ref, ba_ref, bx_ref,
                  lam_ref, rproj_ref, wout_ref, ln_g_ref, ln_b_ref, o_ref,
                  cbuf, rbuf, ubuf, hgbuf, hcar, yb0, ga0, gx0, rc0, yb1, ga1, gx1, rc1,
                  *, alpha, tm, n_tiles, tiles_per_seq, d_conv, d_rnn, d_model,
                  conv_w, rnn_conv_w, gate_starts):
    s = pl.program_id(0)
    t1 = jnp.minimum(s, n_tiles - 1)
    t2 = jnp.maximum(s - 1, 0)

    @pl.when(s == 0)
    def _():
        for ref in (yb1, ga1, gx1, rc1):
            ref[...] = jnp.zeros(ref.shape, ref.dtype)

    @pl.when(t1 % tiles_per_seq == 0)
    def _():
        cbuf[0:CONV_HALO, :] = jnp.zeros((CONV_HALO, d_conv), F32)
        rbuf[0:RNN_HALO, :] = jnp.zeros((RNN_HALO, d_rnn), F32)

    @pl.when(t2 % tiles_per_seq == 0)
    def _():
        hcar[...] = jnp.zeros((SUBLANES, d_rnn), F32)

    def body(wr, rd):
        _mixer_body(x1_ref, x2_ref, w_in_ref, b_in_ref, dw_w_ref, dw_b_ref, gn_g_ref, gn_b_ref,
                    cproj_ref, rc_w_ref, rc_b_ref, wg_ref, ba_ref, bx_ref,
                    lam_ref, rproj_ref, wout_ref, ln_g_ref, ln_b_ref, o_ref,
                    cbuf, rbuf, ubuf, hgbuf, hcar, wr, rd,
                    alpha=alpha, tm=tm, d_conv=d_conv, d_rnn=d_rnn, d_model=d_model,
                    conv_w=conv_w, rnn_conv_w=rnn_conv_w, gate_starts=gate_starts)

    set0 = (yb0, ga0, gx0, rc0)
    set1 = (yb1, ga1, gx1, rc1)

    @pl.when(s % 2 == 0)
    def _():
        body(set0, set1)

    @pl.when(s % 2 == 1)
    def _():
        body(set1, set0)


def _mixer(x2d, w_in, p, layer, alpha, gate_starts, tiles_per_seq):
    m_rows, d = x2d.shape
    tm = MIX_TM
    n_tiles = m_rows // tm
    conv_w, d_conv = p["dw_w"].shape[1:]
    rnn_conv_w, d_rnn = p["rc_w"].shape[1:]
    assert conv_w - 1 <= CONV_HALO and rnn_conv_w - 1 <= RNN_HALO
    names = ["w_in", "b_in", "dw_w", "dw_b", "gn_g", "gn_b", "cproj", "rc_w", "rc_b",
             "wg", "ba", "bx", "lam", "rproj", "wout", "ln_g", "ln_b"]
    args = [w_in] + [p[n] for n in names[1:]]
    in_specs = [pl.BlockSpec((tm, d), lambda s: (jnp.minimum(s, n_tiles - 1), 0)),
                pl.BlockSpec((tm, d), lambda s: (jnp.maximum(s - 1, 0), 0)),
                _resident(w_in.shape)]
    in_specs += [_resident(a.shape[1:], layer) for a in args[1:]]
    kern = functools.partial(
        _mixer_kernel, alpha=alpha, tm=tm, n_tiles=n_tiles, tiles_per_seq=tiles_per_seq,
        d_conv=d_conv, d_rnn=d_rnn, d_model=d, conv_w=conv_w, rnn_conv_w=rnn_conv_w,
        gate_starts=gate_starts)
    handoff = [
        pltpu.VMEM((tm, d_conv), BF16),
        pltpu.VMEM((tm, d_rnn), F32),
        pltpu.VMEM((tm, d_rnn), F32),
        pltpu.VMEM((tm, d_rnn), F32),
    ]
    return pl.pallas_call(
        kern,
        grid=(n_tiles + 1,),
        in_specs=in_specs,
        out_specs=pl.BlockSpec((tm, d), lambda s: (jnp.maximum(s - 1, 0), 0)),
        out_shape=jax.ShapeDtypeStruct((m_rows, d), F32),
        scratch_shapes=[
            pltpu.VMEM((tm + CONV_HALO, d_conv), F32),
            pltpu.VMEM((tm + RNN_HALO, d_rnn), F32),
            pltpu.VMEM((tm, d_rnn + 2 * d), F32),
            pltpu.VMEM((tm, d_rnn), F32),
            pltpu.VMEM((SUBLANES, d_rnn), F32),
        ] + handoff + handoff,
        compiler_params=pltpu.CompilerParams(
            dimension_semantics=("arbitrary",),
            vmem_limit_bytes=VMEM_LIMIT_BYTES),
        name="mixer",
    )(x2d, x2d, *args)


def _gate_weights(w_a, w_x, gate_starts, k_win):
    n_blocks, block, _ = w_a.shape
    d = n_blocks * block
    tiling = (jnp.arange(d)[None, :] % block == jnp.arange(block)[:, None]).astype(w_a.dtype)
    same_block = (jnp.arange(d)[:, None] // block == jnp.arange(d)[None, :] // block)

    def windows(w):
        rows = w.reshape(d, block)
        out = []
        for j, ks in enumerate(gate_starts):
            cols = slice(j * LANES, (j + 1) * LANES)
            full = jnp.dot(rows[ks:ks + k_win], tiling[:, cols], precision=lax.Precision.HIGHEST)
            out.append(jnp.where(same_block[ks:ks + k_win, cols], full, 0.0))
        return jnp.stack(out)

    return jnp.concatenate([windows(w_a), windows(w_x)], axis=-1).astype(BF16)


def kernel(x, ffn1_w_gu, ffn1_w_down, ln1_g, ln1_b, mix_w_in, mix_b_in, conv_dw_w, conv_dw_b, conv_gn_g, conv_gn_b, conv_w_proj, rnn_conv_w, rnn_conv_b, rnn_w_a, rnn_b_a, rnn_w_x, rnn_b_x, rnn_lambda, rnn_w_proj, mix_w_out, ln2_g, ln2_b, ffn2_w_gu, ffn2_w_down, ln3_g, ln3_b):
    bsz, seq, d = x.shape
    depth = ffn1_w_gu.shape[0]
    alpha = float((2 * depth) ** 0.25)
    n_blocks, block = rnn_w_a.shape[1], rnn_w_a.shape[2]
    gate_starts, k_win = _gate_windows(n_blocks, block)

    def row(v):
        return v[:, None, :]

    ffn1_wd = ffn1_w_down.astype(BF16)
    ffn2_wd = ffn2_w_down.astype(BF16)
    mp = {
        "b_in": row(mix_b_in),
        "dw_w": conv_dw_w, "dw_b": row(conv_dw_b),
        "gn_g": row(conv_gn_g), "gn_b": row(conv_gn_b),
        "cproj": conv_w_proj.astype(BF16),
        "rc_w": rnn_conv_w, "rc_b": row(rnn_conv_b),
        "wg": jax.vmap(lambda a, b: _gate_weights(a, b, gate_starts, k_win))(rnn_w_a, rnn_w_x),
        "ba": row(rnn_b_a), "bx": row(rnn_b_x),
        "lam": row(rnn_lambda),
        "rproj": rnn_w_proj.astype(BF16), "wout": mix_w_out.astype(BF16),
        "ln_g": row(ln2_g), "ln_b": row(ln2_b),
    }
    ln1g, ln1b, ln3g, ln3b = row(ln1_g), row(ln1_b), row(ln3_g), row(ln3_b)

    wgu1 = ffn1_w_gu[0].astype(BF16)
    h = x.reshape(bsz * seq, d)
    for l in range(depth):
        h, (w_in, wgu2) = _ffn_ln(h, wgu1, ffn1_wd, ln1g, ln1b, l, alpha,
                                  side=((mix_w_in, l), (ffn2_w_gu, l)))
        h = _mixer(h, w_in, mp, l, alpha, gate_starts, seq // MIX_TM)
        nxt = ((ffn1_w_gu, l + 1),) if l + 1 < depth else ()
        h, cast = _ffn_ln(h, wgu2, ffn2_wd, ln3g, ln3b, l, alpha, side=nxt)
        if cast:
            wgu1 = cast[0]
    return h.reshape(bsz, seq, d)
```

```python
import functools

import jax
import jax.numpy as jnp
from jax import lax
from jax.experimental import pallas as pl
from jax.experimental.pallas import tpu as pltpu

F32 = jnp.float32
BF16 = jnp.bfloat16

LN_EPS = 1e-5
RG_LRU_C = 8.0
LOG2_E = 1.4426950408889634
CONV_GROUPS = 8
SUBLANES = 8
LANES = 128
MXU_COLS = 256
VMEM_LIMIT_BYTES = 56 * 1024 * 1024

FFN_TM = 512
MIX_TM = 256
CONV_ROWS = 64


def _layer_norm(z, g, b):
    mu = jnp.mean(z, axis=-1, keepdims=True)
    zc = z - mu
    var = jnp.mean(zc * zc, axis=-1, keepdims=True)
    return zc * lax.rsqrt(var + LN_EPS) * g + b


def _silu(v):
    return v * jax.nn.sigmoid(v)


def _dot(a, b):
    return jnp.dot(a, b, preferred_element_type=F32)


def _ffn_ln_kernel(*refs, alpha, d_ff, n_side):
    x_ref, wgu_ref, wd_ref, g_ref, b_ref = refs[:5]
    side_in = refs[5:5 + n_side]
    o_ref = refs[5 + n_side]
    side_out = refs[6 + n_side:6 + 2 * n_side]
    x = x_ref[...]
    xb = x.astype(BF16)
    gate = _dot(xb, wgu_ref[:, :d_ff])
    up = _dot(xb, wgu_ref[:, d_ff:])
    act = (_silu(gate) * up).astype(BF16)
    half = x.shape[0] // 2
    for r0 in (0, half):
        y = _dot(act[r0:r0 + half], wd_ref[...])
        z = alpha * x[r0:r0 + half] + 0.5 * y
        o_ref[r0:r0 + half, :] = _layer_norm(z, g_ref[...], b_ref[...])
    for src, dst in zip(side_in, side_out):
        dst[...] = src[...].astype(BF16)


def _resident(shape, layer=None):
    nd = len(shape)
    if layer is None:
        return pl.BlockSpec(tuple(shape), lambda *_: (0,) * nd, pipeline_mode=pl.Buffered(1))
    return pl.BlockSpec((None,) + tuple(shape), lambda *_: (layer,) + (0,) * nd,
                        pipeline_mode=pl.Buffered(1))


def _ffn_ln(x2d, wgu, wd, g, b, layer, alpha, side=()):
    m, d = x2d.shape
    d_ff = wd.shape[1]
    tm = FFN_TM
    n_steps = m // tm
    side_specs_in, side_specs_out, side_shapes = [], [], []
    for w, wl in side:
        rows, cols = w.shape[1:]
        slab = rows // n_steps
        assert slab * n_steps == rows and slab % (2 * SUBLANES) == 0
        side_specs_in.append(pl.BlockSpec((None, slab, cols), lambda i, wl=wl: (wl, i, 0)))
        side_specs_out.append(pl.BlockSpec((slab, cols), lambda i: (i, 0)))
        side_shapes.append(jax.ShapeDtypeStruct((rows, cols), BF16))
    outs = pl.pallas_call(
        functools.partial(_ffn_ln_kernel, alpha=alpha, d_ff=d_ff, n_side=len(side)),
        grid=(n_steps,),
        in_specs=[
            pl.BlockSpec((tm, d), lambda i: (i, 0)),
            _resident((d, 2 * d_ff)),
            _resident((d_ff, d), layer),
            _resident((1, d), layer),
            _resident((1, d), layer),
        ] + side_specs_in,
        out_specs=[pl.BlockSpec((tm, d), lambda i: (i, 0))] + side_specs_out,
        out_shape=[jax.ShapeDtypeStruct((m, d), F32)] + side_shapes,
        compiler_params=pltpu.CompilerParams(
            dimension_semantics=("arbitrary",),
            vmem_limit_bytes=VMEM_LIMIT_BYTES),
        name="ffn_ln",
    )(x2d, wgu, wd, g, b, *[w for w, _ in side])
    return outs[0], outs[1:]


CONV_HALO = 32
RNN_HALO = 8


def _gate_windows(n_blocks, block):
    d = n_blocks * block
    assert d % LANES == 0
    spans = []
    for j in range(d // LANES):
        b0 = (j * LANES) // block
        b1 = (j * LANES + LANES - 1) // block
        spans.append(((b0 * block) // LANES * LANES, (b1 + 1) * block))
    k_win = min(max(-(-(e - s) // LANES) * LANES for s, e in spans), d)
    return tuple(min(s, d - k_win) for s, _ in spans), k_win


def _mixer_body(x1_ref, x2_ref, w_in_ref, b_in_ref, dw_w_ref, dw_b_ref, gn_g_ref, gn_b_ref,
                cproj_ref, rc_w_ref, rc_b_ref, wg_ref, ba_ref, bx_ref,
                lam_ref, rproj_ref, wout_ref, ln_g_ref, ln_b_ref, o_ref,
                cbuf, rbuf, ubuf, hgbuf, hcar, wr, rd,
                *, alpha, tm, d_conv, d_rnn, d_model, conv_w, rnn_conv_w, gate_starts):
    yb_w, ga_w, gx_w, rc_w = wr
    yb_r, ga_r, gx_r, rc_r = rd
    R = CONV_ROWS
    o_cg = d_conv
    o_rx = 2 * d_conv
    o_rg = o_rx + d_rnn
    n_fill = 2 * d_model + d_rnn

    def proj(xb, lo, hi):
        return _dot(xb, w_in_ref[:, lo:hi]) + b_in_ref[:, lo:hi]

    xb1 = x1_ref[...].astype(BF16)
    cbuf[CONV_HALO:CONV_HALO + tm, :] = proj(xb1, 0, o_cg) * jax.nn.sigmoid(proj(xb1, o_cg, o_rx))
    rbuf[RNN_HALO:RNN_HALO + tm, :] = proj(xb1, o_rx, o_rg)

    x2 = x2_ref[...]
    xb2 = x2.astype(BF16)
    for lo in range(0, n_fill, MXU_COLS):
        hi = min(lo + MXU_COLS, n_fill)
        ubuf[:, lo:hi] = proj(xb2, o_rg + lo, o_rg + hi)
    y_conv = _dot(yb_r[...], cproj_ref[...])

    rlead = RNN_HALO - (rnn_conv_w - 1)
    for t in range(d_rnn // LANES):
        ln = slice(t * LANES, (t + 1) * LANES)
        for r0 in range(0, tm, R):
            win = rbuf[r0:r0 + R + RNN_HALO, ln]
            acc = jnp.broadcast_to(rc_b_ref[:, ln], (R, LANES))
            for k in range(rnn_conv_w):
                acc = acc + rc_w_ref[k:k + 1, ln] * win[rlead + k:rlead + k + R]
            rc_w[r0:r0 + R, ln] = acc
    rbuf[0:RNN_HALO, :] = rbuf[tm:tm + RNN_HALO, :]
    rb = rc_w[...].astype(BF16)
    k_win = wg_ref.shape[1]
    for j, ks in enumerate(gate_starts):
        ln = slice(j * LANES, (j + 1) * LANES)
        g2 = _dot(rb[:, ks:ks + k_win], wg_ref[j])
        ga_w[:, ln] = g2[:, :LANES] + ba_ref[:, ln]
        gx_w[:, ln] = g2[:, LANES:] + bx_ref[:, ln]

    lead = CONV_HALO - (conv_w - 1)
    n_win = R + CONV_HALO
    gsz = d_conv // CONV_GROUPS

    def conv_unit(g, r0):
        ln = slice(g * gsz, (g + 1) * gsz)
        win = cbuf[r0:r0 + n_win, ln]
        acc = jnp.broadcast_to(dw_b_ref[:, ln], (R, gsz))
        for sub in range(SUBLANES):
            sh = win if sub == 0 else pltpu.roll(win, n_win - sub, 0)
            for blk in range(CONV_HALO // SUBLANES + 1):
                off = blk * SUBLANES + sub
                k = off - lead
                if 0 <= k < conv_w and off + R <= n_win:
                    acc = acc + dw_w_ref[k:k + 1, ln] * sh[blk * SUBLANES:blk * SUBLANES + R]
        mu = jnp.mean(acc, axis=-1, keepdims=True)
        xc = acc - mu
        var = jnp.mean(xc * xc, axis=-1, keepdims=True)
        yn = xc * lax.rsqrt(var + LN_EPS) * gn_g_ref[:, ln] + gn_b_ref[:, ln]
        yb_w[r0:r0 + R, ln] = _silu(yn).astype(BF16)

    conv_units = [(g, r0) for g in range(CONV_GROUPS) for r0 in range(0, tm, R)]

    neg_c_sp = -RG_LRU_C * jax.nn.softplus(-lam_ref[...])
    neg_c_sp2 = neg_c_sp * LOG2_E
    row = lax.broadcasted_iota(jnp.int32, (SUBLANES, LANES), 0)

    def scan_unit(t):
        ln = slice(t * LANES, (t + 1) * LANES)
        ncs = neg_c_sp[:, ln]
        ncs2 = neg_c_sp2[:, ln]
        h_prev = hcar[:, ln]
        for r0 in range(0, tm, SUBLANES):
            rows = slice(r0, r0 + SUBLANES)
            rgate = jax.nn.sigmoid(ga_r[rows, ln])
            igate = jax.nn.sigmoid(gx_r[rows, ln])
            log_a = rgate * ncs
            a = jnp.exp2(rgate * ncs2)
            w = -jnp.tanh(log_a) * (a * a + 1.0)
            mult = jnp.where(w > 0.0, w * lax.rsqrt(w), 0.0)
            u = mult * (igate * rc_r[rows, ln])
            d = 1
            while d < SUBLANES:
                keep = row >= d
                a_s = jnp.where(keep, pltpu.roll(a, d, 0), 1.0)
                u_s = jnp.where(keep, pltpu.roll(u, d, 0), 0.0)
                u = a * u_s + u
                a = a * a_s
                d *= 2
            h = u + a * h_prev
            hgbuf[rows, ln] = h
            h_prev = jnp.broadcast_to(h[SUBLANES - 1:SUBLANES, :], (SUBLANES, LANES))
        hcar[:, ln] = h_prev

    n_scan = d_rnn // LANES
    tail_units = len(conv_units) // 4
    per_scan = -(-(len(conv_units) - tail_units) // n_scan)
    done = 0
    for t in range(n_scan):
        scan_unit(t)
        for _ in range(per_scan):
            if done < len(conv_units) - tail_units:
                conv_unit(*conv_units[done])
                done += 1

    y_rnn = _dot((hgbuf[...] * jax.nn.gelu(ubuf[:, 0:d_rnn])).astype(BF16), rproj_ref[...])
    o_gc = d_rnn
    o_gr = d_rnn + d_model
    half = tm // 2
    for h0 in (0, half):
        hr = slice(h0, h0 + half)
        m = (jax.nn.sigmoid(ubuf[hr, o_gc:o_gr]) * y_conv[hr]
             + jax.nn.sigmoid(ubuf[hr, o_gr:o_gr + d_model]) * y_rnn[hr])
        mix = _dot(m.astype(BF16), wout_ref[...])
        z = alpha * x2[hr] + mix
        o_ref[hr, :] = _layer_norm(z, ln_g_ref[...], ln_b_ref[...])

    while done < len(conv_units):
        conv_unit(*conv_units[done])
        done += 1
    cbuf[0:CONV_HALO, :] = cbuf[tm:tm + CONV_HALO, :]


def _mixer_kernel(x1_ref, x2_ref, w_in_ref, b_in_ref, dw_w_ref, dw_b_ref, gn_g_ref, gn_b_ref,
                  cproj_ref, rc_w_ref, rc_b_ref, wg_ref, ba_ref, bx_ref,
                  lam_ref, rproj_ref, wout_ref, ln_g_ref, ln_b_ref, o_ref,
                  cbuf, rbuf, ubuf, hgbuf, hcar, yb0, ga0, gx0, rc0, yb1, ga1, gx1, rc1,
                  *, alpha, tm, n_tiles, tiles_per_seq, d_conv, d_rnn, d_model,
                  conv_w, rnn_conv_w, gate_starts):
    s = pl.program_id(0)
    t1 = jnp.minimum(s, n_tiles - 1)
    t2 = jnp.maximum(s - 1, 0)

    @pl.when(s == 0)
    def _():
        for ref in (yb1, ga1, gx1, rc1):
            ref[...] = jnp.zeros(ref.shape, ref.dtype)

    @pl.when(t1 % tiles_per_seq == 0)
    def _():
        cbuf[0:CONV_HALO, :] = jnp.zeros((CONV_HALO, d_conv), F32)
        rbuf[0:RNN_HALO, :] = jnp.zeros((RNN_HALO, d_rnn), F32)

    @pl.when(t2 % tiles_per_seq == 0)
    def _():
        hcar[...] = jnp.zeros((SUBLANES, d_rnn), F32)

    def body(wr, rd):
        _mixer_body(x1_ref, x2_ref, w_in_ref, b_in_ref, dw_w_ref, dw_b_ref, gn_g_ref, gn_b_ref,
                    cproj_ref, rc_w_ref, rc_b_ref, wg_ref, ba_ref, bx_ref,
                    lam_ref, rproj_ref, wout_ref, ln_g_ref, ln_b_ref, o_ref,
                    cbuf, rbuf, ubuf, hgbuf, hcar, wr, rd,
                    alpha=alpha, tm=tm, d_conv=d_conv, d_rnn=d_rnn, d_model=d_model,
                    conv_w=conv_w, rnn_conv_w=rnn_conv_w, gate_starts=gate_starts)

    set0 = (yb0, ga0, gx0, rc0)
    set1 = (yb1, ga1, gx1, rc1)

    @pl.when(s % 2 == 0)
    def _():
        body(set0, set1)

    @pl.when(s % 2 == 1)
    def _():
        body(set1, set0)


def _mixer(x2d, w_in, p, layer, alpha, gate_starts, tiles_per_seq):
    m_rows, d = x2d.shape
    tm = MIX_TM
    n_tiles = m_rows // tm
    conv_w, d_conv = p["dw_w"].shape[1:]
    rnn_conv_w, d_rnn = p["rc_w"].shape[1:]
    assert conv_w - 1 <= CONV_HALO and rnn_conv_w - 1 <= RNN_HALO
    names = ["w_in", "b_in", "dw_w", "dw_b", "gn_g", "gn_b", "cproj", "rc_w", "rc_b",
             "wg", "ba", "bx", "lam", "rproj", "wout", "ln_g", "ln_b"]
    args = [w_in] + [p[n] for n in names[1:]]
    in_specs = [pl.BlockSpec((tm, d), lambda s: (jnp.minimum(s, n_tiles - 1), 0)),
                pl.BlockSpec((tm, d), lambda s: (jnp.maximum(s - 1, 0), 0)),
                _resident(w_in.shape)]
    in_specs += [_resident(a.shape[1:], layer) for a in args[1:]]
    kern = functools.partial(
        _mixer_kernel, alpha=alpha, tm=tm, n_tiles=n_tiles, tiles_per_seq=tiles_per_seq,
        d_conv=d_conv, d_rnn=d_rnn, d_model=d, conv_w=conv_w, rnn_conv_w=rnn_conv_w,
        gate_starts=gate_starts)
    handoff = [
        pltpu.VMEM((tm, d_conv), BF16),
        pltpu.VMEM((tm, d_rnn), F32),
        pltpu.VMEM((tm, d_rnn), F32),
        pltpu.VMEM((tm, d_rnn), F32),
    ]
    return pl.pallas_call(
        kern,
        grid=(n_tiles + 1,),
        in_specs=in_specs,
        out_specs=pl.BlockSpec((tm, d), lambda s: (jnp.maximum(s - 1, 0), 0)),
        out_shape=jax.ShapeDtypeStruct((m_rows, d), F32),
        scratch_shapes=[
            pltpu.VMEM((tm + CONV_HALO, d_conv), F32),
            pltpu.VMEM((tm + RNN_HALO, d_rnn), F32),
            pltpu.VMEM((tm, d_rnn + 2 * d), F32),
            pltpu.VMEM((tm, d_rnn), F32),
            pltpu.VMEM((SUBLANES, d_rnn), F32),
        ] + handoff + handoff,
        compiler_params=pltpu.CompilerParams(
            dimension_semantics=("arbitrary",),
            vmem_limit_bytes=VMEM_LIMIT_BYTES),
        name="mixer",
    )(x2d, x2d, *args)


def _gate_weights(w_a, w_x, gate_starts, k_win):
    n_blocks, block, _ = w_a.shape
    d = n_blocks * block
    tiling = (jnp.arange(d)[None, :] % block == jnp.arange(block)[:, None]).astype(w_a.dtype)
    same_block = (jnp.arange(d)[:, None] // block == jnp.arange(d)[None, :] // block)

    def windows(w):
        rows = w.reshape(d, block)
        out = []
        for j, ks in enumerate(gate_starts):
            cols = slice(j * LANES, (j + 1) * LANES)
            full = jnp.dot(rows[ks:ks + k_win], tiling[:, cols], precision=lax.Precision.HIGHEST)
            out.append(jnp.where(same_block[ks:ks + k_win, cols], full, 0.0))
        return jnp.stack(out)

    return jnp.concatenate([windows(w_a), windows(w_x)], axis=-1).astype(BF16)


def kernel(x, ffn1_w_gu, ffn1_w_down, ln1_g, ln1_b, mix_w_in, mix_b_in, conv_dw_w, conv_dw_b, conv_gn_g, conv_gn_b, conv_w_proj, rnn_conv_w, rnn_conv_b, rnn_w_a, rnn_b_a, rnn_w_x, rnn_b_x, rnn_lambda, rnn_w_proj, mix_w_out, ln2_g, ln2_b, ffn2_w_gu, ffn2_w_down, ln3_g, ln3_b):
    bsz, seq, d = x.shape
    depth = ffn1_w_gu.shape[0]
    alpha = float((2 * depth) ** 0.25)
    n_blocks, block = rnn_w_a.shape[1], rnn_w_a.shape[2]
    gate_starts, k_win = _gate_windows(n_blocks, block)

    def row(v):
        return v[:, None, :]

    ffn1_wd = ffn1_w_down.astype(BF16)
    ffn2_wd = ffn2_w_down.astype(BF16)
    mp = {
        "b_in": row(mix_b_in),
        "dw_w": conv_dw_w, "dw_b": row(conv_dw_b),
        "gn_g": row(conv_gn_g), "gn_b": row(conv_gn_b),
        "cproj": conv_w_proj.astype(BF16),
        "rc_w": rnn_conv_w, "rc_b": row(rnn_conv_b),
        "wg": jax.vmap(lambda a, b: _gate_weights(a, b, gate_starts, k_win))(rnn_w_a, rnn_w_x),
        "ba": row(rnn_b_a), "bx": row(rnn_b_x),
        "lam": row(rnn_lambda),
        "rproj": rnn_w_proj.astype(BF16), "wout": mix_w_out.astype(BF16),
        "ln_g": row(ln2_g), "ln_b": row(ln2_b),
    }
    ln1g, ln1b, ln3g, ln3b = row(ln1_g), row(ln1_b), row(ln3_g), row(ln3_b)

    wgu1 = ffn1_w_gu[0].astype(BF16)
    h = x.reshape(bsz * seq, d)
    for l in range(depth):
        h, (w_in, wgu2) = _ffn_ln(h, wgu1, ffn1_wd, ln1g, ln1b, l, alpha,
                                  side=((mix_w_in, l), (ffn2_w_gu, l)))
        h = _mixer(h, w_in, mp, l, alpha, gate_starts, seq // MIX_TM)
        nxt = ((ffn1_w_gu, l + 1),) if l + 1 < depth else ()
        h, cast = _ffn_ln(h, wgu2, ffn2_wd, ln3g, ln3b, l, alpha, side=nxt)
        if cast:
            wgu1 = cast[0]
    return h.reshape(bsz, seq, d)
```
